```python
import jax
import jax.numpy as jnp
from jax import lax
import numpy as np

D_MODEL = 1024
BATCH = 2
SEQ = 16384
DEPTH = 4

N_MIXERS = 3
N_ATTN = (DEPTH + 2) // 3
N_CONV = (DEPTH + 1) // 3
N_RWKV = DEPTH // 3
NORM_EPS = 1e-6

ATTN_HEADS = 16
ATTN_HEAD_DIM = 64
DILATED_GROUPS = ((128, 1), (512, 4), (2048, 16))
N_GROUPS = len(DILATED_GROUPS)
ATTN_BLOCK = 128
ROPE_THETA = 10000.0
NEG_INF = -1e30

CONV_WIDTH = 31
CONV_CH = D_MODEL

RWKV_HEAD_DIM = 64
RWKV_HEADS = D_MODEL // RWKV_HEAD_DIM
DECAY_LORA = 64
AAA_LORA = 64
GATE_LORA = 128
RWKV_GN_EPS = 1e-5 * RWKV_HEAD_DIM

FFN_HIDDEN = ((8 * D_MODEL // 3 + 255) // 256) * 256

kernel_name = "hybrid_dilated_conv_rwkv7_adaln_trunk"


def rms_norm(t, g, eps=NORM_EPS):
    t32 = t.astype(jnp.float32)
    return (t32 * lax.rsqrt(jnp.mean(t32 * t32, -1, keepdims=True) + eps) * g).astype(t.dtype)


def layer_norm(t, g, b, eps=NORM_EPS):
    t32 = t.astype(jnp.float32)
    mu = jnp.mean(t32, -1, keepdims=True)
    var = jnp.mean(jnp.square(t32 - mu), -1, keepdims=True)
    return ((t32 - mu) * lax.rsqrt(var + eps) * g + b).astype(t.dtype)


def rope(t, cos, sin):
    t1, t2 = jnp.split(t, 2, axis=-1)
    return jnp.concatenate([t1 * cos - t2 * sin, t2 * cos + t1 * sin], -1).astype(t.dtype)


def dilated_window_attention(q, k, v, dilation, steps):
    B, S, H, E = q.shape
    L = S // dilation
    nb = -(-L // ATTN_BLOCK)
    Lp = nb * ATTN_BLOCK

    def to_blocks(t):
        t = t.reshape(B, L, dilation, H, E).transpose(0, 2, 1, 3, 4)
        t = jnp.pad(t, ((0, 0), (0, 0), (0, Lp - L), (0, 0), (0, 0)))
        return t.reshape(B, dilation, nb, ATTN_BLOCK, H, E)

    def with_prev(t):
        prev = jnp.pad(t[:, :, :-1], ((0, 0), (0, 0), (1, 0), (0, 0), (0, 0), (0, 0)))
        return jnp.concatenate([prev, t], axis=3)

    qb = to_blocks(q)
    kb = with_prev(to_blocks(k))
    vb = with_prev(to_blocks(v))
    s = jnp.einsum('brnihe,brnjhe->brnhij', qb, kb).astype(jnp.float32) * (E ** -0.5)
    i = jnp.arange(ATTN_BLOCK)[:, None]
    j = jnp.arange(2 * ATTN_BLOCK)[None, :]
    dist = i + ATTN_BLOCK - j
    blk = jnp.arange(nb)[:, None, None]
    valid = (dist >= 0) & (dist <= steps) & (blk * ATTN_BLOCK + j >= ATTN_BLOCK)
    s = jnp.where(valid[None, None, :, None], s, NEG_INF)
    lse = jax.nn.logsumexp(s, axis=-1)
    p = jnp.exp(s - lse[..., None]).astype(v.dtype)
    o = jnp.einsum('brnhij,brnjhe->brnihe', p, vb)

    def from_blocks(t):
        t = t.reshape((B, dilation, Lp) + t.shape[4:])[:, :, :L]
        return jnp.swapaxes(t, 1, 2).reshape((B, S) + t.shape[3:])

    return from_blocks(o), from_blocks(jnp.swapaxes(lse, 3, 4))


def dilated_attention_mixer(h, cos, sin, w_qkv, q_gain, k_gain, w_o):
    B, S, _ = h.shape
    qkv = (h @ w_qkv).reshape(B, S, N_GROUPS, 3, ATTN_HEADS, ATTN_HEAD_DIM)
    q = rope(rms_norm(qkv[:, :, :, 0], q_gain), cos, sin)
    k = rope(rms_norm(qkv[:, :, :, 1], k_gain), cos, sin)
    v = qkv[:, :, :, 2]
    outs, lses = [], []
    for g, (window, dilation) in enumerate(DILATED_GROUPS):
        o_g, lse_g = dilated_window_attention(q[:, :, g], k[:, :, g], v[:, :, g], dilation, window // dilation)
        outs.append(o_g)
        lses.append(lse_g)
    wts = jax.nn.softmax(jnp.stack(lses), axis=0)
    o = jnp.einsum('gbsh,gbshe->bshe', wts.astype(v.dtype), jnp.stack(outs))
    return o.reshape(B, S, ATTN_HEADS * ATTN_HEAD_DIM) @ w_o


def conformer_conv_mixer(h, w_pw1, b_pw1, w_dw, b_dw, ln_g, ln_b, w_pw2, b_pw2):
    u = jax.nn.glu(h @ w_pw1 + b_pw1, axis=-1)
    u = lax.conv_general_dilated(
        u, w_dw[:, None, :].astype(u.dtype), window_strides=(1,),
        padding=((CONV_WIDTH - 1, 0),),
        dimension_numbers=('NWC', 'WIO', 'NWC'), feature_group_count=CONV_CH) + b_dw
    u = layer_norm(u, ln_g, ln_b)
    return jax.nn.silu(u) @ w_pw2 + b_pw2


def wkv7_scan(r, decay, k, v, a_vec, b_vec):
    B, S, H, N = r.shape

    def step(state, inp):
        r_t, d_t, k_t, v_t, a_t, b_t = inp
        sa = jnp.einsum('bhij,bhj->bhi', state, a_t)
        state = state * d_t[:, :, None, :] + sa[..., None] * b_t[:, :, None, :] + v_t[..., None] * k_t[:, :, None, :]
        return state, jnp.einsum('bhij,bhj->bhi', state, r_t)

    xs = tuple(jnp.moveaxis(t, 1, 0) for t in (r, decay, k, v, a_vec, b_vec))
    _, ys = lax.scan(step, jnp.zeros((B, H, N, N), jnp.float32), xs)
    return jnp.moveaxis(ys, 0, 1)


def rwkv7_mixer(h, mu, w_r, w_k, w_v, w_o, w0, w_lora_a, w_lora_b, a0, a_lora_a, a_lora_b,
                g_lora_a, g_lora_b, k_k, k_a, r_k, ln_g, ln_b):
    B, S, D = h.shape
    xx = jnp.pad(h[:, :-1], ((0, 0), (1, 0), (0, 0))) - h
    xr, xw, xk, xv, xa, xg = (h + xx * mu[m] for m in range(6))
    r = xr @ w_r
    k = xk @ w_k
    v = xv @ w_v
    w = -jax.nn.softplus(-(w0 + jnp.tanh(xw @ w_lora_a) @ w_lora_b)) - 0.5
    a = jax.nn.sigmoid(a0 + (xa @ a_lora_a) @ a_lora_b)
    g = jax.nn.sigmoid(xg @ g_lora_a) @ g_lora_b

    def heads(t):
        return t.reshape(B, S, RWKV_HEADS, RWKV_HEAD_DIM).astype(jnp.float32)

    kk = heads(k * k_k)
    kk = kk / jnp.maximum(jnp.sqrt(jnp.sum(kk * kk, -1, keepdims=True)), 1e-12)
    k = k * (1 + (a - 1) * k_a)
    r, k, v, a = heads(r), heads(k), heads(v), heads(a)
    decay = jnp.exp(-jnp.exp(heads(w)))
    y = wkv7_scan(r, decay, k, v, -kk, kk * a)
    mean = jnp.mean(y, -1, keepdims=True)
    var = jnp.mean(jnp.square(y - mean), -1, keepdims=True)
    y = ((y - mean) * lax.rsqrt(var + RWKV_GN_EPS)).reshape(B, S, D) * ln_g + ln_b
    bonus = (jnp.sum(r * k * r_k, -1, keepdims=True) * v).reshape(B, S, D)
    return ((y + bonus) * g).astype(h.dtype) @ w_o


def swiglu_ffn(h, w_gate, w_up, w_down):
    return (jax.nn.silu(h @ w_gate) * (h @ w_up)) @ w_down


def setup_inputs(seed: int = 0) -> dict:
    key = jax.random.key(seed)
    ks = iter(jax.random.split(key, 48))

    def nrm(shape, scale):
        return jax.random.normal(next(ks), shape, jnp.float32) * scale

    def gain(shape):
        return 1.0 + nrm(shape, 0.02)

    D, F = D_MODEL, FFN_HIDDEN
    HE = ATTN_HEADS * ATTN_HEAD_DIM
    x = nrm((BATCH, SEQ, D), 1.0)
    c = nrm((BATCH, D), 1.0)
    offset = jax.random.randint(next(ks), (BATCH, 1), 0, 4096, dtype=jnp.int32)
    positions = offset + jnp.arange(SEQ, dtype=jnp.int32)[None, :]
    return {
        "x": x,
        "c": c,
        "positions": positions,
        "ada_w": nrm((DEPTH, D, 6 * D), 0.02),
        "ada_b": nrm((DEPTH, 6 * D), 0.02),
        "norm_mix_g": gain((DEPTH, D)),
        "norm_ffn_g": gain((DEPTH, D)),
        "ffn_w_gate": nrm((DEPTH, D, F), D ** -0.5),
        "ffn_w_up": nrm((DEPTH, D, F), D ** -0.5),
        "ffn_w_down": nrm((DEPTH, F, D), F ** -0.5),
        "attn_w_qkv": nrm((N_ATTN, D, N_GROUPS * 3 * HE), D ** -0.5),
        "attn_q_gain": gain((N_ATTN, ATTN_HEAD_DIM)),
        "attn_k_gain": gain((N_ATTN, ATTN_HEAD_DIM)),
        "attn_w_o": nrm((N_ATTN, HE, D), HE ** -0.5),
        "conv_w_pw1": nrm((N_CONV, D, 2 * CONV_CH), D ** -0.5),
        "conv_b_pw1": nrm((N_CONV, 2 * CONV_CH), 0.02),
        "conv_w_dw": nrm((N_CONV, CONV_WIDTH, CONV_CH), CONV_WIDTH ** -0.5),
        "conv_b_dw": nrm((N_CONV, CONV_CH), 0.02),
        "conv_ln_g": gain((N_CONV, CONV_CH)),
        "conv_ln_b": nrm((N_CONV, CONV_CH), 0.02),
        "conv_w_pw2": nrm((N_CONV, CONV_CH, D), CONV_CH ** -0.5),
        "conv_b_pw2": nrm((N_CONV, D), 0.02),
        "rwkv_mu": jax.random.uniform(next(ks), (N_RWKV, 6, D), jnp.float32),
        "rwkv_w_r": nrm((N_RWKV, D, D), D ** -0.5),
        "rwkv_w_k": nrm((N_RWKV, D, D), D ** -0.5),
        "rwkv_w_v": nrm((N_RWKV, D, D), D ** -0.5),
        "rwkv_w_o": nrm((N_RWKV, D, D), D ** -0.5),
        "rwkv_w0": jax.random.uniform(next(ks), (N_RWKV, D), jnp.float32, -5.0, 1.0),
        "rwkv_w_lora_a": nrm((N_RWKV, D, DECAY_LORA), D ** -0.5),
        "rwkv_w_lora_b": nrm((N_RWKV, DECAY_LORA, D), 0.1 * DECAY_LORA ** -0.5),
        "rwkv_a0": nrm((N_RWKV, D), 0.1),
        "rwkv_a_lora_a": nrm((N_RWKV, D, AAA_LORA), D ** -0.5),
        "rwkv_a_lora_b": nrm((N_RWKV, AAA_LORA, D), 0.1 * AAA_LORA ** -0.5),
        "rwkv_g_lora_a": nrm((N_RWKV, D, GATE_LORA), D ** -0.5),
        "rwkv_g_lora_b": nrm((N_RWKV, GATE_LORA, D), GATE_LORA ** -0.5),
        "rwkv_k_k": 0.85 + nrm((N_RWKV, D), 0.02),
        "rwkv_k_a": gain((N_RWKV, D)),
        "rwkv_r_k": nrm((N_RWKV, RWKV_HEADS, RWKV_HEAD_DIM), 0.1),
        "rwkv_ln_g": gain((N_RWKV, D)),
        "rwkv_ln_b": nrm((N_RWKV, D), 0.02),
    }


def reference(x, c, positions, ada_w, ada_b, norm_mix_g, norm_ffn_g, ffn_w_gate, ffn_w_up, ffn_w_down,
              attn_w_qkv, attn_q_gain, attn_k_gain, attn_w_o,
              conv_w_pw1, conv_b_pw1, conv_w_dw, conv_b_dw, conv_ln_g, conv_ln_b, conv_w_pw2, conv_b_pw2,
              rwkv_mu, rwkv_w_r, rwkv_w_k, rwkv_w_v, rwkv_w_o, rwkv_w0, rwkv_w_lora_a, rwkv_w_lora_b,
              rwkv_a0, rwkv_a_lora_a, rwkv_a_lora_b, rwkv_g_lora_a, rwkv_g_lora_b,
              rwkv_k_k, rwkv_k_a, rwkv_r_k, rwkv_ln_g, rwkv_ln_b):
    B, S, D = x.shape
    inv_freq = jnp.power(ROPE_THETA, -jnp.arange(0, ATTN_HEAD_DIM, 2, dtype=jnp.float32) / ATTN_HEAD_DIM)
    ang = positions.astype(jnp.float32)[..., None] * inv_freq
    cos = jnp.cos(ang)[:, :, None, None]
    sin = jnp.sin(ang)[:, :, None, None]
    c_act = jax.nn.silu(c)
    for i in range(DEPTH):
        mod = (c_act @ ada_w[i] + ada_b[i]).reshape(B, 6, D)
        shift_m, scale_m, gate_m, shift_f, scale_f, gate_f = (mod[:, m, None, :] for m in range(6))
        h = rms_norm(x, norm_mix_g[i]) * (1 + scale_m) + shift_m
        kind, j = i % N_MIXERS, i // N_MIXERS
        if kind == 0:
            y = dilated_attention_mixer(h, cos, sin, attn_w_qkv[j], attn_q_gain[j], attn_k_gain[j], attn_w_o[j])
        elif kind == 1:
            y = conformer_conv_mixer(h, conv_w_pw1[j], conv_b_pw1[j], conv_w_dw[j], conv_b_dw[j],
                                     conv_ln_g[j], conv_ln_b[j], conv_w_pw2[j], conv_b_pw2[j])
        else:
            y = rwkv7_mixer(h, rwkv_mu[j], rwkv_w_r[j], rwkv_w_k[j], rwkv_w_v[j], rwkv_w_o[j],
                            rwkv_w0[j], rwkv_w_lora_a[j], rwkv_w_lora_b[j],
                            rwkv_a0[j], rwkv_a_lora_a[j], rwkv_a_lora_b[j],
                            rwkv_g_lora_a[j], rwkv_g_lora_b[j],
                            rwkv_k_k[j], rwkv_k_a[j], rwkv_r_k[j], rwkv_ln_g[j], rwkv_ln_b[j])
        x = x + gate_m * y
        h = rms_norm(x, norm_ffn_g[i]) * (1 + scale_f) + shift_f
        x = x + gate_f * swiglu_ffn(h, ffn_w_gate[i], ffn_w_up[i], ffn_w_down[i])
    return x
```

```python
import functools

import jax
import jax.numpy as jnp
from jax import lax
from jax.experimental import pallas as pl
from jax.experimental.pallas import tpu as pltpu

F32 = jnp.float32
BF16 = jnp.bfloat16

NORM_EPS = 1e-6
HEAD_DIM = 64
ATTN_HEADS = 16
DILATED_GROUPS = ((128, 1), (512, 4), (2048, 16))
ATTN_BLOCK = 128
ROPE_THETA = 10000.0
NEG_INF = -1e30
CONV_WIDTH = 31
RWKV_GN_EPS = 1e-5 * HEAD_DIM
CHUNK = 64
LANE_GROUP = 256
VMEM_LIMIT = 56 * 1024 * 1024


def _cparams(sem):
    return pltpu.CompilerParams(dimension_semantics=sem, vmem_limit_bytes=VMEM_LIMIT)


def _dot(a, b):
    return jnp.dot(a, b, preferred_element_type=F32)


def _dot_t(a, b):
    return lax.dot_general(a, b, (((1,), (1,)), ((), ())), preferred_element_type=F32)


def _split_dot(x, w):
    hi = x.astype(BF16)
    lo = (x - hi.astype(F32)).astype(BF16)
    return _dot(hi, w) + _dot(lo, w)


def _sigmoid(x):
    return 1.0 / (1.0 + jnp.exp(-x))


def _normmod(x, g, scale, shift):
    ms = jnp.mean(x * x, axis=-1, keepdims=True)
    return x * lax.rsqrt(ms + NORM_EPS) * g * (1.0 + scale) + shift


HEAD_SHIFT = HEAD_DIM.bit_length() - 1
assert 1 << HEAD_SHIFT == HEAD_DIM


def _head_of(shape, axis):
    return jnp.right_shift(lax.broadcasted_iota(jnp.int32, shape, axis), HEAD_SHIFT)


def _pos_in_head(shape, axis):
    return jnp.bitwise_and(lax.broadcasted_iota(jnp.int32, shape, axis), HEAD_DIM - 1)


def _head_ones(n):
    return (_head_of((n, n), 0) == _head_of((n, n), 1)).astype(BF16)


def _ada_kernel(c_ref, w_ref, b_ref, o_ref):
    c = c_ref[...]
    c_act = (c * _sigmoid(c)).astype(BF16)
    o_ref[...] = _dot(c_act, w_ref[...].astype(BF16)) + b_ref[...]


def _ada_mod(c, ada_w, ada_b):
    depth, d, n = ada_w.shape
    bsz = c.shape[0]
    rows = 8
    c_pad = jnp.zeros((rows, d), F32).at[:bsz].set(c)
    tn = 1536
    out = pl.pallas_call(
        _ada_kernel,
        grid=(depth, n // tn),
        in_specs=[
            pl.BlockSpec((rows, d), lambda i, j: (0, 0)),
            pl.BlockSpec((None, d, tn), lambda i, j: (i, 0, j)),
            pl.BlockSpec((None, 1, tn), lambda i, j: (i, 0, j)),
        ],
        out_specs=pl.BlockSpec((None, rows, tn), lambda i, j: (i, 0, j)),
        out_shape=jax.ShapeDtypeStruct((depth, rows, n), F32),
        compiler_params=_cparams(("parallel", "parallel")),
        name="ada_mod",
    )(c_pad, ada_w, ada_b.reshape(depth, 1, n))
    return out[:, :bsz].reshape(depth, bsz, 6, d)


def _ffn_kernel(x_ref, mod_ref, g_ref, wg_ref, wu_ref, wd_ref, o_ref, h_scr, acc_scr):
    j = pl.program_id(2)

    @pl.when(j == 0)
    def _():
        h = _normmod(x_ref[...], g_ref[...], mod_ref[4:5, :], mod_ref[3:4, :])
        h_scr[...] = h.astype(BF16)
        acc_scr[...] = jnp.zeros_like(acc_scr)

    h = h_scr[...]
    a = _dot(h, wg_ref[...])
    u = _dot(h, wu_ref[...])
    z = (a * _sigmoid(a) * u).astype(BF16)
    acc_scr[...] += _dot(z, wd_ref[...])

    @pl.when(j == pl.num_programs(2) - 1)
    def _():
        o_ref[...] = x_ref[...] + mod_ref[5:6, :] * acc_scr[...]


def _ffn(x, mod, g, wg, wu, wd):
    bsz, s, d = x.shape
    f = wg.shape[1]
    tm, tf = 1024, 256
    return pl.pallas_call(
        _ffn_kernel,
        grid=(bsz, s // tm, f // tf),
        in_specs=[
            pl.BlockSpec((None, tm, d), lambda b, i, j: (b, i, 0)),
            pl.BlockSpec((None, 6, d), lambda b, i, j: (b, 0, 0)),
            pl.BlockSpec((1, d), lambda b, i, j: (0, 0)),
            pl.BlockSpec((d, tf), lambda b, i, j: (0, j)),
            pl.BlockSpec((d, tf), lambda b, i, j: (0, j)),
            pl.BlockSpec((tf, d), lambda b, i, j: (j, 0)),
        ],
        out_specs=pl.BlockSpec((None, tm, d), lambda b, i, j: (b, i, 0)),
        out_shape=jax.ShapeDtypeStruct(x.shape, F32),
        scratch_shapes=[pltpu.VMEM((tm, d), BF16), pltpu.VMEM((tm, d), F32)],
        compiler_params=_cparams(("parallel", "parallel", "arbitrary")),
        name="ffn",
    )(x, mod, g.reshape(1, d), wg, wu, wd)


def _proj_res_kernel(z_ref, w_ref, b_ref, x_ref, mod_ref, o_ref):
    y = _dot(z_ref[...], w_ref[...]) + b_ref[...]
    o_ref[...] = x_ref[...] + mod_ref[2:3, :] * y


def _proj_residual(z, w, bias, x, mod):
    bsz, s, d = x.shape
    k = z.shape[-1]
    tm = 1024
    return pl.pallas_call(
        _proj_res_kernel,
        grid=(bsz, s // tm),
        in_specs=[
            pl.BlockSpec((None, tm, k), lambda b, i: (b, i, 0)),
            pl.BlockSpec((k, d), lambda b, i: (0, 0)),
            pl.BlockSpec((1, d), lambda b, i: (0, 0)),
            pl.BlockSpec((None, tm, d), lambda b, i: (b, i, 0)),
            pl.BlockSpec((None, 6, d), lambda b, i: (b, 0, 0)),
        ],
        out_specs=pl.BlockSpec((None, tm, d), lambda b, i: (b, i, 0)),
        out_shape=jax.ShapeDtypeStruct(x.shape, F32),
        compiler_params=_cparams(("parallel", "parallel")),
        name="proj_residual",
    )(z, w, bias.reshape(1, d), x, mod)


def _qkv_kernel(x_ref, mod_ref, g_ref, w_ref, gain_ref, cos_ref, sin_ref, o_ref, h_scr):
    j = pl.program_id(2)

    @pl.when(j == 0)
    def _():
        h = _normmod(x_ref[...], g_ref[...], mod_ref[1:2, :], mod_ref[0:1, :])
        h_scr[...] = h.astype(BF16)

    y = _dot(h_scr[...], w_ref[...])
    kind = j % 3

    @pl.when(kind == 2)
    def _():
        o_ref[...] = y.astype(o_ref.dtype)

    @pl.when(kind != 2)
    def _():
        n = y.shape[1]
        ones = _head_ones(LANE_GROUP)
        sq = y * y
        ss = jnp.concatenate(
            [_split_dot(sq[:, c:c + LANE_GROUP], ones) for c in range(0, n, LANE_GROUP)], axis=1)
        t = y * lax.rsqrt(ss * (1.0 / HEAD_DIM) + NORM_EPS) * gain_ref[...]
        reps = n // cos_ref.shape[1]
        cos = jnp.concatenate([cos_ref[...]] * reps, axis=1)
        sin = jnp.concatenate([sin_ref[...]] * reps, axis=1)
        lane = _pos_in_head(t.shape, 1)
        half = HEAD_DIM // 2
        partner = jnp.where(lane < half, pltpu.roll(t, n - half, 1), pltpu.roll(t, half, 1))
        o_ref[...] = (t * cos + partner * sin).astype(o_ref.dtype)


def _qkv_proj(x, mod, g, w_qkv, gains, cos_t, sin_t):
    bsz, s, d = x.shape
    n = w_qkv.shape[1]
    tn = ATTN_HEADS * HEAD_DIM
    tm = 1024
    tw = cos_t.shape[-1]
    return pl.pallas_call(
        _qkv_kernel,
        grid=(bsz, s // tm, n // tn),
        in_specs=[
            pl.BlockSpec((None, tm, d), lambda b, i, j: (b, i, 0)),
            pl.BlockSpec((None, 6, d), lambda b, i, j: (b, 0, 0)),
            pl.BlockSpec((1, d), lambda b, i, j: (0, 0)),
            pl.BlockSpec((d, tn), lambda b, i, j: (0, j)),
            pl.BlockSpec((None, 1, tn), lambda b, i, j: (j % 3, 0, 0)),
            pl.BlockSpec((None, tm, tw), lambda b, i, j: (b, i, 0)),
            pl.BlockSpec((None, tm, tw), lambda b, i, j: (b, i, 0)),
        ],
        out_specs=pl.BlockSpec((None, tm, tn), lambda b, i, j: (b, i, j)),
        out_shape=jax.ShapeDtypeStruct((bsz, s, n), BF16),
        scratch_shapes=[pltpu.VMEM((tm, d), BF16)],
        compiler_params=_cparams(("parallel", "parallel", "arbitrary")),
        name="attn_qkv",
    )(x, mod, g.reshape(1, d), w_qkv, gains, cos_t, sin_t)


def _attn_kernel(q_ref, kc_ref, kp_ref, vc_ref, vp_ref, o_ref, lse_ref):
    n = pl.program_id(2)
    blk = ATTN_BLOCK
    row = lax.broadcasted_iota(jnp.int32, (blk, blk), 0)
    col = lax.broadcasted_iota(jnp.int32, (blk, blk), 1)
    cur_ok = col <= row
    prev_ok = col >= row
    prev_bias = jnp.where(n > 0, 0.0, NEG_INF)
    pair = 2 * HEAD_DIM
    lane = lax.broadcasted_iota(jnp.int32, (blk, pair), 1)
    first = lane < HEAD_DIM
    lse_lane = lax.broadcasted_iota(jnp.int32, (blk, lse_ref.shape[1]), 1)
    lse_tile = jnp.zeros((blk, lse_ref.shape[1]), F32)
    scale = HEAD_DIM ** -0.5
    zero = jnp.zeros((), BF16)
    for hp in range(ATTN_HEADS // 2):
        sl = slice(hp * pair, (hp + 1) * pair)
        q2, kc2, kp2, vc2, vp2 = q_ref[:, sl], kc_ref[:, sl], kp_ref[:, sl], vc_ref[:, sl], vp_ref[:, sl]
        outs = []
        for sub in range(2):
            keep = first if sub == 0 else jnp.logical_not(first)
            qh = jnp.where(keep, q2, zero)
            s_c = jnp.where(cur_ok, _dot_t(qh, kc2) * scale, NEG_INF)
            s_p = jnp.where(prev_ok, _dot_t(qh, kp2) * scale + prev_bias, NEG_INF)
            m = jnp.maximum(jnp.max(s_c, axis=-1, keepdims=True), jnp.max(s_p, axis=-1, keepdims=True))
            p_c = jnp.exp(s_c - m)
            p_p = jnp.exp(s_p - m)
            l = jnp.sum(p_c, axis=-1, keepdims=True) + jnp.sum(p_p, axis=-1, keepdims=True)
            o = (_dot(p_c.astype(BF16), vc2) + _dot(p_p.astype(BF16), vp2)) / l
            outs.append(o)
            lse_tile = jnp.where(lse_lane == 2 * hp + sub, m + jnp.log(l), lse_tile)
        o_ref[:, sl] = jnp.where(first, outs[0], outs[1]).astype(o_ref.dtype)
    lse_ref[...] = lse_tile


def _dilated_attention(qkv, group, dilation):
    bsz, s, n = qkv.shape
    he = ATTN_HEADS * HEAD_DIM
    nchunks = n // he
    l_sub = s // dilation
    nb = l_sub // ATTN_BLOCK
    view = qkv.reshape(bsz, l_sub, dilation * n)
    base = 3 * group
    lse_w = 128

    def spec(which, prev):
        def imap(b, r, i):
            blk = jnp.maximum(i - 1, 0) if prev else i
            return (b, blk, r * nchunks + base + which)
        return pl.BlockSpec((None, ATTN_BLOCK, he), imap)

    o, lse = pl.pallas_call(
        _attn_kernel,
        grid=(bsz, dilation, nb),
        in_specs=[spec(0, False), spec(1, False), spec(1, True), spec(2, False), spec(2, True)],
        out_specs=[
            pl.BlockSpec((None, ATTN_BLOCK, he), lambda b, r, i: (b, i, r)),
            pl.BlockSpec((None, ATTN_BLOCK, lse_w), lambda b, r, i: (b, i, r)),
        ],
        out_shape=[
            jax.ShapeDtypeStruct((bsz, l_sub, dilation * he), BF16),
            jax.ShapeDtypeStruct((bsz, l_sub, dilation * lse_w), F32),
        ],
        compiler_params=_cparams(("parallel", "parallel", "parallel")),
        name=f"dilated_attn_g{group}",
    )(view, view, view, view, view)
    return o.reshape(bsz, s, he), lse.reshape(bsz, s, lse_w)


def _merge_kernel(o0_ref, o1_ref, o2_ref, l0_ref, l1_ref, l2_ref, e_ref, w_ref, x_ref, mod_ref, out_ref):
    l0, l1, l2 = l0_ref[...], l1_ref[...], l2_ref[...]
    m = jnp.maximum(jnp.maximum(l0, l1), l2)
    e0, e1, e2 = jnp.exp(l0 - m), jnp.exp(l1 - m), jnp.exp(l2 - m)
    inv = 1.0 / (e0 + e1 + e2)
    expand = e_ref[...]
    o = (_split_dot(e0 * inv, expand) * o0_ref[...].astype(F32)
         + _split_dot(e1 * inv, expand) * o1_ref[...].astype(F32)
         + _split_dot(e2 * inv, expand) * o2_ref[...].astype(F32))
    y = _dot(o.astype(BF16), w_ref[...])
    out_ref[...] = x_ref[...] + mod_ref[2:3, :] * y


def _attn_merge_proj(outs, lses, w_o, x, mod):
    bsz, s, d = x.shape
    he = outs[0].shape[-1]
    lw = lses[0].shape[-1]
    tm = 512
    head_of_lane = jnp.arange(he, dtype=jnp.int32) // HEAD_DIM
    expand = (jnp.arange(lw, dtype=jnp.int32)[:, None] == head_of_lane[None, :]).astype(BF16)
    o_spec = pl.BlockSpec((None, tm, he), lambda b, i: (b, i, 0))
    l_spec = pl.BlockSpec((None, tm, lw), lambda b, i: (b, i, 0))
    return pl.pallas_call(
        _merge_kernel,
        grid=(bsz, s // tm),
        in_specs=[o_spec, o_spec, o_spec, l_spec, l_spec, l_spec,
                  pl.BlockSpec((lw, he), lambda b, i: (0, 0)),
                  pl.BlockSpec((he, d), lambda b, i: (0, 0)),
                  pl.BlockSpec((None, tm, d), lambda b, i: (b, i, 0)),
                  pl.BlockSpec((None, 6, d), lambda b, i: (b, 0, 0))],
        out_specs=pl.BlockSpec((None, tm, d), lambda b, i: (b, i, 0)),
        out_shape=jax.ShapeDtypeStruct(x.shape, F32),
        compiler_params=_cparams(("parallel", "parallel")),
        name="attn_merge_proj",
    )(*outs, *lses, expand, w_o, x, mod)


def _attention_layer(x, mod, g, w_qkv, q_gain, k_gain, w_o, cos_t, sin_t):
    he = ATTN_HEADS * HEAD_DIM
    gains = jnp.stack([jnp.tile(q_gain, ATTN_HEADS), jnp.tile(k_gain, ATTN_HEADS),
                       jnp.ones((he,), F32)]).reshape(3, 1, he)
    qkv = _qkv_proj(x, mod, g, w_qkv, gains, cos_t, sin_t)
    outs, lses = [], []
    for grp, (_, dilation) in enumerate(DILATED_GROUPS):
        o, lse = _dilated_attention(qkv, grp, dilation)
        outs.append(o)
        lses.append(lse)
    return _attn_merge_proj(outs, lses, w_o, x, mod)


def _pw1_glu_kernel(x_ref, mod_ref, g_ref, wa_ref, wb_ref, ba_ref, bb_ref, o_ref, h_scr):
    j = pl.program_id(2)

    @pl.when(j == 0)
    def _():
        h = _normmod(x_ref[...], g_ref[...], mod_ref[1:2, :], mod_ref[0:1, :])
        h_scr[...] = h.astype(BF16)

    h = h_scr[...]
    a = _dot(h, wa_ref[...]) + ba_ref[...]
    b = _dot(h, wb_ref[...]) + bb_ref[...]
    o_ref[...] = a * _sigmoid(b)


def _pw1_glu(x, mod, g, w_pw1, b_pw1):
    bsz, s, d = x.shape
    ch = w_pw1.shape[1] // 2
    tm, tn = 1024, 512
    nj = ch // tn
    b2 = b_pw1.reshape(1, 2 * ch)
    return pl.pallas_call(
        _pw1_glu_kernel,
        grid=(bsz, s // tm, nj),
        in_specs=[
            pl.BlockSpec((None, tm, d), lambda b, i, j: (b, i, 0)),
            pl.BlockSpec((None, 6, d), lambda b, i, j: (b, 0, 0)),
            pl.BlockSpec((1, d), lambda b, i, j: (0, 0)),
            pl.BlockSpec((d, tn), lambda b, i, j: (0, j)),
            pl.BlockSpec((d, tn), lambda b, i, j: (0, j + nj)),
            pl.BlockSpec((1, tn), lambda b, i, j: (0, j)),
            pl.BlockSpec((1, tn), lambda b, i, j: (0, j + nj)),
        ],
        out_specs=pl.BlockSpec((None, tm, tn), lambda b, i, j: (b, i, j)),
        out_shape=jax.ShapeDtypeStruct((bsz, s, ch), F32),
        scratch_shapes=[pltpu.VMEM((tm, d), BF16)],
        compiler_params=_cparams(("parallel", "parallel", "arbitrary")),
        name="conv_pw1_glu",
    )(x, mod, g.reshape(1, d), w_pw1, w_pw1, b2, b2)


CONV_HALO = 32
CONV_ROWS = 16


def _dwconv_kernel(u_ref, halo_ref, wdw_ref, bdw_ref, lng_ref, lnb_ref, w2_ref, b2_ref, x_ref, mod_ref,
                   o_ref, ubuf, cbuf):
    i = pl.program_id(1)
    tm = u_ref.shape[0]
    halo = halo_ref[...]
    ubuf[0:CONV_HALO, :] = jnp.where(i > 0, halo, jnp.zeros_like(halo))
    ubuf[CONV_HALO:, :] = u_ref[...]
    first = CONV_HALO - (CONV_WIDTH - 1)
    for c in range(tm // CONV_ROWS):
        base = c * CONV_ROWS + first
        acc = jnp.zeros((CONV_ROWS, u_ref.shape[1]), F32)
        for k in range(CONV_WIDTH):
            acc = acc + wdw_ref[k:k + 1, :] * ubuf[base + k:base + k + CONV_ROWS, :]
        cbuf[c * CONV_ROWS:(c + 1) * CONV_ROWS, :] = acc
    u = cbuf[...] + bdw_ref[...]
    mu = jnp.mean(u, axis=-1, keepdims=True)
    uc = u - mu
    var = jnp.mean(uc * uc, axis=-1, keepdims=True)
    v = uc * lax.rsqrt(var + NORM_EPS) * lng_ref[...] + lnb_ref[...]
    z = (v * _sigmoid(v)).astype(BF16)
    y = _dot(z, w2_ref[...]) + b2_ref[...]
    o_ref[...] = x_ref[...] + mod_ref[2:3, :] * y


def _dwconv_ln_pw2(u, w_dw, b_dw, ln_g, ln_b, w_pw2, b_pw2, x, mod):
    bsz, s, d = x.shape
    ch = u.shape[-1]
    tm = 256
    ratio = tm // CONV_HALO
    wpad = jnp.zeros((32, ch), F32).at[:CONV_WIDTH].set(w_dw)
    row = lambda a: a.reshape(1, -1)
    return pl.pallas_call(
        _dwconv_kernel,
        grid=(bsz, s // tm),
        in_specs=[
            pl.BlockSpec((None, tm, ch), lambda b, i: (b, i, 0)),
            pl.BlockSpec((None, CONV_HALO, ch), lambda b, i: (b, jnp.maximum(i * ratio - 1, 0), 0)),
            pl.BlockSpec((32, ch), lambda b, i: (0, 0)),
            pl.BlockSpec((1, ch), lambda b, i: (0, 0)),
            pl.BlockSpec((1, ch), lambda b, i: (0, 0)),
            pl.BlockSpec((1, ch), lambda b, i: (0, 0)),
            pl.BlockSpec((ch, d), lambda b, i: (0, 0)),
            pl.BlockSpec((1, d), lambda b, i: (0, 0)),
            pl.BlockSpec((None, tm, d), lambda b, i: (b, i, 0)),
            pl.BlockSpec((None, 6, d), lambda b, i: (b, 0, 0)),
        ],
        out_specs=pl.BlockSpec((None, tm, d), lambda b, i: (b, i, 0)),
        out_shape=jax.ShapeDtypeStruct(x.shape, F32),
        scratch_shapes=[pltpu.VMEM((tm + CONV_HALO, ch), F32), pltpu.VMEM((tm, ch), F32)],
        compiler_params=_cparams(("parallel", "parallel")),
        name="conv_dw_ln_pw2",
    )(u, u, wpad, row(b_dw), row(ln_g), row(ln_b), w_pw2, row(b_pw2), x, mod)


def _conv_layer(x, mod, g, w_pw1, b_pw1, w_dw, b_dw, ln_g, ln_b, w_pw2, b_pw2):
    u = _pw1_glu(x, mod, g, w_pw1, b_pw1)
    return _dwconv_ln_pw2(u, w_dw, b_dw, ln_g, ln_b, w_pw2, b_pw2, x, mod)


def _rwkv_proj_kernel(x_ref, halo_ref, mod_ref, g_ref, mu_ref, wr_ref, wk_ref, wv_ref,
                      wa_ref, wb_ref, aa_ref, ab_ref, ga_ref, gb_ref, vec_ref,
                      r_ref, ld_ref, k_ref, v_ref, kk_ref, a_ref, gate_ref):
    i = pl.program_id(1)
    g = g_ref[...]
    scale, shift = mod_ref[1:2, :], mod_ref[0:1, :]
    h = _normmod(x_ref[...], g, scale, shift)
    h_halo = _normmod(halo_ref[...], g, scale, shift)
    last = jnp.where(i > 0, h_halo[7:8, :], jnp.zeros_like(h_halo[7:8, :]))
    rows = lax.broadcasted_iota(jnp.int32, h.shape, 0)
    h_prev = jnp.where(rows == 0, last, pltpu.roll(h, 1, 0))
    xx = h_prev - h

    def mix(m):
        return (h + xx * mu_ref[m:m + 1, :]).astype(BF16)

    w0, a0, k_k, k_a = vec_ref[0:1, :], vec_ref[1:2, :], vec_ref[2:3, :], vec_ref[3:4, :]
    r = _dot(mix(0), wr_ref[...])
    k = _dot(mix(2), wk_ref[...])
    v = _dot(mix(3), wv_ref[...])
    zw = w0 + _dot(jnp.tanh(_dot(mix(1), wa_ref[...])).astype(BF16), wb_ref[...])
    w = -(jnp.maximum(-zw, 0.0) + jnp.log1p(jnp.exp(-jnp.abs(zw)))) - 0.5
    a = _sigmoid(a0 + _dot(_dot(mix(4), aa_ref[...]).astype(BF16), ab_ref[...]))
    gate = _dot(_sigmoid(_dot(mix(5), ga_ref[...])).astype(BF16), gb_ref[...])
    kk = k * k_k
    ones = _head_ones(LANE_GROUP)
    sq = kk * kk
    d = kk.shape[1]
    ss = jnp.concatenate(
        [_split_dot(sq[:, c:c + LANE_GROUP], ones) for c in range(0, d, LANE_GROUP)], axis=1)
    kk = kk / jnp.maximum(jnp.sqrt(ss), 1e-12)
    r_ref[...] = r
    ld_ref[...] = -jnp.exp(w)
    k_ref[...] = k * (1.0 + (a - 1.0) * k_a)
    v_ref[...] = v
    kk_ref[...] = kk
    a_ref[...] = a
    gate_ref[...] = gate


def _rwkv_proj(x, mod, g, mu, w_r, w_k, w_v, wa, wb, aa, ab, ga, gb, vecs):
    bsz, s, d = x.shape
    tm = 256
    ratio = tm // 8
    full = lambda a: pl.BlockSpec(a.shape, lambda b, i: (0,) * a.ndim)
    tok = pl.BlockSpec((None, tm, d), lambda b, i: (b, i, 0))
    mu8 = jnp.zeros((8, d), F32).at[:6].set(mu)
    weights = [w_r, w_k, w_v, wa, wb, aa, ab, ga, gb, vecs]
    return pl.pallas_call(
        _rwkv_proj_kernel,
        grid=(bsz, s // tm),
        in_specs=[tok,
                  pl.BlockSpec((None, 8, d), lambda b, i: (b, jnp.maximum(i * ratio - 1, 0), 0)),
                  pl.BlockSpec((None, 6, d), lambda b, i: (b, 0, 0)),
                  pl.BlockSpec((1, d), lambda b, i: (0, 0)),
                  full(mu8)] + [full(a) for a in weights],
        out_specs=[tok] * 7,
        out_shape=[jax.ShapeDtypeStruct(x.shape, F32)] * 7,
        compiler_params=_cparams(("parallel", "parallel")),
        name="rwkv_proj",
    )(x, x, mod, g.reshape(1, d), mu8, *weights)


def _block_rows(m, head_lane_masks):
    return jnp.concatenate([jnp.where(msk, m, 0.0) for msk in head_lane_masks], axis=0)


def _wkv_kernel(r_ref, ld_ref, k_ref, v_ref, kk_ref, a_ref, gate_ref, vec_ref, o_ref, state):
    t = pl.program_id(2)

    @pl.when(t == 0)
    def _():
        state[...] = jnp.zeros_like(state)

    c = CHUNK
    w = LANE_GROUP
    nh = w // HEAD_DIM
    n_chunks = r_ref.shape[0] // c
    hi = lax.Precision.HIGHEST

    lane_head = _head_of((c, w), 1)
    head_masks = [lane_head == h for h in range(nh)]
    row_c = lax.broadcasted_iota(jnp.int32, (c, w), 0)
    assert c == HEAD_DIM
    col_in = _pos_in_head((c, w), 1)
    strict = col_in < row_c
    incl = col_in <= row_c
    eye = (col_in == row_c).astype(F32)
    tri = (lax.broadcasted_iota(jnp.int32, (c, c), 1) <= lax.broadcasted_iota(jnp.int32, (c, c), 0)).astype(F32)
    diag_blocks = _head_of((w, w), 0) == _head_of((w, w), 1)
    ones = diag_blocks.astype(BF16)
    r_k, ln_g, ln_b = vec_ref[0:1, :], vec_ref[1:2, :], vec_ref[2:3, :]

    def bd(m):
        return _block_rows(m, head_masks)

    def mm(a, b):
        return jnp.dot(a, b, preferred_element_type=F32, precision=hi)

    def mm_t(a, b):
        return lax.dot_general(a, b, (((1,), (1,)), ((), ())), preferred_element_type=F32, precision=hi)

    def chunk_body(ci, carry):
        rows = pl.ds(pl.multiple_of(ci * c, c), c)
        r, ld, k, v = r_ref[rows, :], ld_ref[rows, :], k_ref[rows, :], v_ref[rows, :]
        kk, a, gate = kk_ref[rows, :], a_ref[rows, :], gate_ref[rows, :]
        s_bd = state[...]

        cum = mm(tri, ld)
        p_in = jnp.exp(cum)
        p_ex = jnp.exp(cum - ld)
        p_inv = jnp.exp(-cum)
        p_end = p_in[c - 1:c, :]
        at = -kk * p_ex
        bt = kk * a * p_inv
        kt = k * p_inv
        rt = r * p_in

        lhs = jnp.concatenate([at, rt], axis=0)
        g_b = mm_t(lhs, bd(bt))
        g_k = mm_t(lhs, bd(kt))
        a_ab = jnp.where(strict, g_b[:c], 0.0)
        a_ak = jnp.where(strict, g_k[:c], 0.0)
        b_rb = jnp.where(incl, g_b[c:], 0.0)
        b_rk = jnp.where(incl, g_k[c:], 0.0)

        x_inv = eye + a_ab
        pw = a_ab
        for _ in range(5):
            pw = mm(pw, jnp.where(diag_blocks, jnp.concatenate([pw] * nh, axis=0), 0.0))
            x_inv = x_inv + mm(x_inv, jnp.where(diag_blocks, jnp.concatenate([pw] * nh, axis=0), 0.0))

        sa_sr = mm_t(lhs, s_bd)
        v_bd = bd(v)
        u_rhs = sa_sr[:c] + mm(a_ak, v_bd)
        u = mm(x_inv, bd(u_rhs))
        y = sa_sr[c:] + mm(jnp.concatenate([b_rb, b_rk], axis=1), jnp.concatenate([bd(u), v_bd], axis=0))

        uv = jnp.concatenate([u, v], axis=0)
        bk = jnp.concatenate([bt * p_end, kt * p_end], axis=0)
        upd = mm(uv.T, bk)
        state[...] = s_bd * p_end + jnp.where(diag_blocks, upd, 0.0)

        inv_n = 1.0 / HEAD_DIM
        mean = _split_dot(y, ones) * inv_n
        yc = y - mean
        var = _split_dot(yc * yc, ones) * inv_n
        yn = yc * lax.rsqrt(var + RWKV_GN_EPS) * ln_g + ln_b
        bonus = _split_dot(r * k * r_k, ones) * v
        o_ref[rows, :] = ((yn + bonus) * gate).astype(o_ref.dtype)
        return carry

    lax.fori_loop(0, n_chunks, chunk_body, 0)


def _wkv_scan(r, ld, k, v, kk, a, gate, vecs):
    bsz, s, d = r.shape
    tb = 512
    w = LANE_GROUP
    tok = pl.BlockSpec((None, tb, w), lambda b, gi, t: (b, t, gi))
    return pl.pallas_call(
        _wkv_kernel,
        grid=(bsz, d // w, s // tb),
        in_specs=[tok] * 7 + [pl.BlockSpec((8, w), lambda b, gi, t: (0, gi))],
        out_specs=tok,
        out_shape=jax.ShapeDtypeStruct((bsz, s, d), BF16),
        scratch_shapes=[pltpu.VMEM((w, w), F32)],
        compiler_params=_cparams(("parallel", "parallel", "arbitrary")),
        name="wkv7_scan",
    )(r, ld, k, v, kk, a, gate, vecs)


def _rwkv_layer(x, mod, g, mu, w_r, w_k, w_v, w_o, w0, wa, wb, a0, aa, ab, ga, gb, k_k, k_a, r_k, ln_g, ln_b):
    d = x.shape[-1]
    vecs = jnp.zeros((8, d), F32).at[0].set(w0).at[1].set(a0).at[2].set(k_k).at[3].set(k_a)
    r, ld, k, v, kk, a, gate = _rwkv_proj(x, mod, g, mu, w_r, w_k, w_v, wa, wb, aa, ab, ga, gb, vecs)
    svecs = jnp.zeros((8, d), F32).at[0].set(r_k.reshape(d)).at[1].set(ln_g).at[2].set(ln_b)
    z = _wkv_scan(r, ld, k, v, kk, a, gate, svecs)
    return _proj_residual(z, w_o, jnp.zeros((d,), F32), x, mod)


def _rope_tables(positions):
    half = HEAD_DIM // 2
    inv_freq = jnp.power(ROPE_THETA, -jnp.arange(0, HEAD_DIM, 2, dtype=F32) / HEAD_DIM)
    ang = positions.astype(F32)[..., None] * inv_freq
    cos, sin = jnp.cos(ang), jnp.sin(ang)
    assert cos.shape[-1] == half
    return jnp.concatenate([cos, cos, cos, cos], -1), jnp.concatenate([-sin, sin, -sin, sin], -1)


def kernel(x, c, positions, ada_w, ada_b, norm_mix_g, norm_ffn_g, ffn_w_gate, ffn_w_up, ffn_w_down, attn_w_qkv, attn_q_gain, attn_k_gain, attn_w_o, conv_w_pw1, conv_b_pw1, conv_w_dw, conv_b_dw, conv_ln_g, conv_ln_b, conv_w_pw2, conv_b_pw2, rwkv_mu, rwkv_w_r, rwkv_w_k, rwkv_w_v, rwkv_w_o, rwkv_w0, rwkv_w_lora_a, rwkv_w_lora_b, rwkv_a0, rwkv_a_lora_a, rwkv_a_lora_b, rwkv_g_lora_a, rwkv_g_lora_b, rwkv_k_k, rwkv_k_a, rwkv_r_k, rwkv_ln_g, rwkv_ln_b):
    depth = ada_w.shape[0]
    bf = lambda a: a.astype(BF16)
    cos_t, sin_t = _rope_tables(positions)
    mods = _ada_mod(c, ada_w, ada_b)
    for i in range(depth):
        mod = mods[i]
        kind, j = i % 3, i // 3
        if kind == 0:
            x = _attention_layer(x, mod, norm_mix_g[i], bf(attn_w_qkv[j]), attn_q_gain[j], attn_k_gain[j],
                                 bf(attn_w_o[j]), cos_t, sin_t)
        elif kind == 1:
            x = _conv_layer(x, mod, norm_mix_g[i], bf(conv_w_pw1[j]), conv_b_pw1[j], conv_w_dw[j], conv_b_dw[j],
                            conv_ln_g[j], conv_ln_b[j], bf(conv_w_pw2[j]), conv_b_pw2[j])
        else:
            x = _rwkv_layer(x, mod, norm_mix_g[i], rwkv_mu[j], bf(rwkv_w_r[j]), bf(rwkv_w_k[j]), bf(rwkv_w_v[j]),
                            bf(rwkv_w_o[j]), rwkv_w0[j], bf(rwkv_w_lora_a[j]), bf(rwkv_w_lora_b[j]), rwkv_a0[j],
                            bf(rwkv_a_lora_a[j]), bf(rwkv_a_lora_b[j]), bf(rwkv_g_lora_a[j]), bf(rwkv_g_lora_b[j]),
                            rwkv_k_k[j], rwkv_k_a[j], rwkv_r_k[j], rwkv_ln_g[j], rwkv_ln_b[j])
        x = _ffn(x, mod, norm_ffn_g[i], bf(ffn_w_gate[i]), bf(ffn_w_up[i]), bf(ffn_w_down[i]))
    return x
```

```python
import functools

import jax
import jax.numpy as jnp
from jax import lax
from jax.experimental import pallas as pl
from jax.experimental.pallas import tpu as pltpu

F32 = jnp.float32
BF16 = jnp.bfloat16

NORM_EPS = 1e-6
HEAD_DIM = 64
ATTN_HEADS = 16
DILATED_GROUPS = ((128, 1), (512, 4), (2048, 16))
ATTN_BLOCK = 128
ROPE_THETA = 10000.0
NEG_INF = -1e30
CONV_WIDTH = 31
RWKV_GN_EPS = 1e-5 * HEAD_DIM
CHUNK = 64
LANE_GROUP = 256
VMEM_LIMIT = 56 * 1024 * 1024


def _cparams(sem):
    return pltpu.CompilerParams(dimension_semantics=sem, vmem_limit_bytes=VMEM_LIMIT)


def _dot(a, b):
    return jnp.dot(a, b, preferred_element_type=F32)


def _dot_t(a, b):
    return lax.dot_general(a, b, (((1,), (1,)), ((), ())), preferred_element_type=F32)


def _split_dot(x, w):
    hi = x.astype(BF16)
    lo = (x - hi.astype(F32)).astype(BF16)
    return _dot(hi, w) + _dot(lo, w)


def _sigmoid(x):
    return 1.0 / (1.0 + jnp.exp(-x))


def _normmod(x, g, scale, shift):
    ms = jnp.mean(x * x, axis=-1, keepdims=True)
    return x * lax.rsqrt(ms + NORM_EPS) * g * (1.0 + scale) + shift


HEAD_SHIFT = HEAD_DIM.bit_length() - 1
assert 1 << HEAD_SHIFT == HEAD_DIM


def _head_of(shape, axis):
    return jnp.right_shift(lax.broadcasted_iota(jnp.int32, shape, axis), HEAD_SHIFT)


def _pos_in_head(shape, axis):
    return jnp.bitwise_and(lax.broadcasted_iota(jnp.int32, shape, axis), HEAD_DIM - 1)


def _head_ones(n):
    return (_head_of((n, n), 0) == _head_of((n, n), 1)).astype(BF16)


def _ada_kernel(c_ref, w_ref, b_ref, o_ref):
    c = c_ref[...]
    c_act = (c * _sigmoid(c)).astype(BF16)
    o_ref[...] = _dot(c_act, w_ref[...].astype(BF16)) + b_ref[...]


def _ada_mod(c, ada_w, ada_b):
    depth, d, n = ada_w.shape
    bsz = c.shape[0]
    rows = 8
    c_pad = jnp.zeros((rows, d), F32).at[:bsz].set(c)
    tn = 1536
    out = pl.pallas_call(
        _ada_kernel,
        grid=(depth, n // tn),
        in_specs=[
            pl.BlockSpec((rows, d), lambda i, j: (0, 0)),
            pl.BlockSpec((None, d, tn), lambda i, j: (i, 0, j)),
            pl.BlockSpec((None, 1, tn), lambda i, j: (i, 0, j)),
        ],
        out_specs=pl.BlockSpec((None, rows, tn), lambda i, j: (i, 0, j)),
        out_shape=jax.ShapeDtypeStruct((depth, rows, n), F32),
        compiler_params=_cparams(("parallel", "parallel")),
        name="ada_mod",
    )(c_pad, ada_w, ada_b.reshape(depth, 1, n))
    return out[:, :bsz].reshape(depth, bsz, 6, d)


def _ffn_kernel(x_ref, mod_ref, g_ref, wg_ref, wu_ref, wd_ref, o_ref, h_scr, acc_scr):
    j = pl.program_id(2)

    @pl.when(j == 0)
    def _():
        h = _normmod(x_ref[...], g_ref[...], mod_ref[4:5, :], mod_ref[3:4, :])
        h_scr[...] = h.astype(BF16)
        acc_scr[...] = jnp.zeros_like(acc_scr)

    h = h_scr[...]
    a = _dot(h, wg_ref[...])
    u = _dot(h, wu_ref[...])
    z = (a * _sigmoid(a) * u).astype(BF16)
    acc_scr[...] += _dot(z, wd_ref[...])

    @pl.when(j == pl.num_programs(2) - 1)
    def _():
        o_ref[...] = x_ref[...] + mod_ref[5:6, :] * acc_scr[...]


def _ffn(x, mod, g, wg, wu, wd):
    bsz, s, d = x.shape
    f = wg.shape[1]
    tm, tf = 1024, 256
    return pl.pallas_call(
        _ffn_kernel,
        grid=(bsz, s // tm, f // tf),
        in_specs=[
            pl.BlockSpec((None, tm, d), lambda b, i, j: (b, i, 0)),
            pl.BlockSpec((None, 6, d), lambda b, i, j: (b, 0, 0)),
            pl.BlockSpec((1, d), lambda b, i, j: (0, 0)),
            pl.BlockSpec((d, tf), lambda b, i, j: (0, j)),
            pl.BlockSpec((d, tf), lambda b, i, j: (0, j)),
            pl.BlockSpec((tf, d), lambda b, i, j: (j, 0)),
        ],
        out_specs=pl.BlockSpec((None, tm, d), lambda b, i, j: (b, i, 0)),
        out_shape=jax.ShapeDtypeStruct(x.shape, F32),
        scratch_shapes=[pltpu.VMEM((tm, d), BF16), pltpu.VMEM((tm, d), F32)],
        compiler_params=_cparams(("parallel", "parallel", "arbitrary")),
        name="ffn",
    )(x, mod, g.reshape(1, d), wg, wu, wd)


def _proj_res_kernel(z_ref, w_ref, b_ref, x_ref, mod_ref, o_ref):
    y = _dot(z_ref[...], w_ref[...]) + b_ref[...]
    o_ref[...] = x_ref[...] + mod_ref[2:3, :] * y


def _proj_residual(z, w, bias, x, mod):
    bsz, s, d = x.shape
    k = z.shape[-1]
    tm = 1024
    return pl.pallas_call(
        _proj_res_kernel,
        grid=(bsz, s // tm),
        in_specs=[
            pl.BlockSpec((None, tm, k), lambda b, i: (b, i, 0)),
            pl.BlockSpec((k, d), lambda b, i: (0, 0)),
            pl.BlockSpec((1, d), lambda b, i: (0, 0)),
            pl.BlockSpec((None, tm, d), lambda b, i: (b, i, 0)),
            pl.BlockSpec((None, 6, d), lambda b, i: (b, 0, 0)),
        ],
        out_specs=pl.BlockSpec((None, tm, d), lambda b, i: (b, i, 0)),
        out_shape=jax.ShapeDtypeStruct(x.shape, F32),
        compiler_params=_cparams(("parallel", "parallel")),
        name="proj_residual",
    )(z, w, bias.reshape(1, d), x, mod)


PLANES = DILATED_GROUPS[-1][1]
ATTN_TM = 1024


LANES = 128


def _to_planes(val, scr, planes):
    tm, n = val.shape
    rows = tm // planes
    for c in range(n // LANES):
        scr[c] = val[:, c * LANES:(c + 1) * LANES]
    return [jnp.concatenate([scr[c, pl.ds(p, rows, stride=planes), :] for c in range(n // LANES)], axis=1)
            for p in range(planes)]


def _from_planes(val, scr, planes):
    tm, n = val.shape
    rows = tm // planes
    for p in range(planes):
        for c in range(n // LANES):
            scr[c, pl.ds(p, rows, stride=planes), :] = val[p * rows:(p + 1) * rows, c * LANES:(c + 1) * LANES]
    return jnp.concatenate([scr[c] for c in range(n // LANES)], axis=1)


def _rms_rope(y, gain, cos, sin):
    n = y.shape[1]
    ones = _head_ones(LANE_GROUP)
    sq = (y * y).astype(BF16)
    ss = jnp.concatenate([_dot(sq[:, c:c + LANE_GROUP], ones) for c in range(0, n, LANE_GROUP)], axis=1)
    t = y * lax.rsqrt(ss * (1.0 / HEAD_DIM) + NORM_EPS) * gain
    reps = n // cos.shape[1]
    cos = jnp.concatenate([cos] * reps, axis=1)
    sin = jnp.concatenate([sin] * reps, axis=1)
    lane = _pos_in_head(t.shape, 1)
    half = HEAD_DIM // 2
    partner = jnp.where(lane < half, pltpu.roll(t, n - half, 1), pltpu.roll(t, half, 1))
    return t * cos + partner * sin


def _qkv_kernel(x_ref, mod_ref, g_ref, w_ref, gain_ref, cos_ref, sin_ref, o_ref, h_scr, hf_scr, *, planes):
    j = pl.program_id(2)
    tm = x_ref.shape[0]

    @pl.when(j == 0)
    def _():
        h = _normmod(x_ref[...], g_ref[...], mod_ref[1:2, :], mod_ref[0:1, :])
        if planes == 1:
            h_scr[...] = h.astype(BF16)
        else:
            rows = tm // planes
            for p, hp in enumerate(_to_planes(h, hf_scr, planes)):
                h_scr[p * rows:(p + 1) * rows, :] = hp.astype(BF16)

    y = _dot(h_scr[...], w_ref[...])
    kind = j % 3

    @pl.when(kind == 2)
    def _():
        o_ref[...] = y.astype(o_ref.dtype).reshape(o_ref.shape)

    @pl.when(kind != 2)
    def _():
        cos = cos_ref[...].reshape(tm, cos_ref.shape[-1])
        sin = sin_ref[...].reshape(tm, sin_ref.shape[-1])
        o_ref[...] = _rms_rope(y, gain_ref[...], cos, sin).astype(o_ref.dtype).reshape(o_ref.shape)


def _qkv_proj(x, mod, g, w, gains, cos_t, sin_t, planes):
    bsz, s, d = x.shape
    n = w.shape[1]
    tn = ATTN_HEADS * HEAD_DIM
    tm = ATTN_TM
    tw = cos_t.shape[-1]
    if planes == 1:
        tok = lambda width: pl.BlockSpec((None, tm, width), lambda b, i, j: (b, i, 0))
        out_spec = pl.BlockSpec((None, tm, tn), lambda b, i, j: (b, i, j))
        out_shape = jax.ShapeDtypeStruct((bsz, s, n), BF16)
    else:
        rows = tm // planes
        tok = lambda width: pl.BlockSpec((None, planes, rows, width), lambda b, i, j: (b, 0, i, 0))
        out_spec = pl.BlockSpec((None, planes, rows, tn), lambda b, i, j: (b, 0, i, j))
        out_shape = jax.ShapeDtypeStruct((bsz, planes, s // planes, n), BF16)
    return pl.pallas_call(
        functools.partial(_qkv_kernel, planes=planes),
        grid=(bsz, s // tm, n // tn),
        in_specs=[
            pl.BlockSpec((None, tm, d), lambda b, i, j: (b, i, 0)),
            pl.BlockSpec((None, 6, d), lambda b, i, j: (b, 0, 0)),
            pl.BlockSpec((1, d), lambda b, i, j: (0, 0)),
            pl.BlockSpec((d, tn), lambda b, i, j: (0, j)),
            pl.BlockSpec((None, 1, tn), lambda b, i, j: (j % 3, 0, 0)),
            tok(tw), tok(tw),
        ],
        out_specs=out_spec,
        out_shape=out_shape,
        scratch_shapes=[pltpu.VMEM((tm, d), BF16), pltpu.VMEM((d // LANES, tm, LANES), F32)],
        compiler_params=_cparams(("parallel", "parallel", "arbitrary")),
        name=f"attn_qkv_p{planes}",
    )(x, mod, g.reshape(1, d), w, gains, cos_t, sin_t)


def _attn_kernel(q_ref, kc_ref, kp_ref, vc_ref, vp_ref, o_ref, lse_ref, *, interleave):
    n = pl.program_id(2)
    blk = ATTN_BLOCK
    he = ATTN_HEADS * HEAD_DIM

    def pos(axis):
        i = lax.broadcasted_iota(jnp.int32, (blk, blk), axis)
        if interleave == 1:
            return i
        seg = blk // interleave
        return interleave * jnp.bitwise_and(i, seg - 1) + jnp.right_shift(i, seg.bit_length() - 1)

    qpos, kpos = pos(0), pos(1)
    cur_ok = kpos <= qpos
    prev_ok = kpos >= qpos
    prev_bias = jnp.where(n > 0, 0.0, NEG_INF)
    pair = 2 * HEAD_DIM
    lane = lax.broadcasted_iota(jnp.int32, (blk, pair), 1)
    first = lane < HEAD_DIM
    scale = HEAD_DIM ** -0.5
    zero = jnp.zeros((), BF16)
    q_all = q_ref[...].reshape(blk, he)
    kc_all, kp_all = kc_ref[...].reshape(blk, he), kp_ref[...].reshape(blk, he)
    vc_all, vp_all = vc_ref[...].reshape(blk, he), vp_ref[...].reshape(blk, he)

    heads = [(hp, sub) for hp in range(ATTN_HEADS // 2) for sub in range(2)]
    sl = lambda hp: slice(hp * pair, (hp + 1) * pair)
    qh = [jnp.where(first if sub == 0 else jnp.logical_not(first), q_all[:, sl(hp)], zero) for hp, sub in heads]
    s_c = [jnp.where(cur_ok, _dot_t(q, kc_all[:, sl(hp)]) * scale, NEG_INF) for q, (hp, _) in zip(qh, heads)]
    s_p = [jnp.where(prev_ok, _dot_t(q, kp_all[:, sl(hp)]) * scale + prev_bias, NEG_INF)
           for q, (hp, _) in zip(qh, heads)]
    m = [jnp.max(jnp.maximum(a, b), axis=-1, keepdims=True) for a, b in zip(s_c, s_p)]
    p_c = [jnp.exp(a - mx) for a, mx in zip(s_c, m)]
    p_p = [jnp.exp(a - mx) for a, mx in zip(s_p, m)]
    l = [jnp.sum(a + b, axis=-1, keepdims=True) for a, b in zip(p_c, p_p)]
    pv = [_dot(a.astype(BF16), vc_all[:, sl(hp)]) + _dot(b.astype(BF16), vp_all[:, sl(hp)])
          for a, b, (hp, _) in zip(p_c, p_p, heads)]
    o = [x / d for x, d in zip(pv, l)]
    lse_lane = lax.broadcasted_iota(jnp.int32, (blk, lse_ref.shape[-1]), 1)
    lse_tile = jnp.zeros((blk, lse_ref.shape[-1]), F32)
    for idx, (mx, d) in enumerate(zip(m, l)):
        lse_tile = jnp.where(lse_lane == idx, mx + jnp.log(d), lse_tile)
    out = jnp.concatenate([jnp.where(first, o[2 * hp], o[2 * hp + 1]) for hp in range(ATTN_HEADS // 2)], axis=1)
    o_ref[...] = out.astype(o_ref.dtype).reshape(o_ref.shape)
    lse_ref[...] = lse_tile.reshape(lse_ref.shape)


LSE_LANES = 128


def _dilated_attention(qkv, group):
    he = ATTN_HEADS * HEAD_DIM
    blk = ATTN_BLOCK
    dilation = DILATED_GROUPS[group][1]
    if group == 0:
        bsz, s, _ = qkv.shape
        grid = (bsz, 1, s // blk)
        block = lambda width: (None, blk, width)
        imap = lambda col, prev: (lambda b, r, i: (b, jnp.maximum(i - 1, 0) if prev else i, col))
        out_dims = (bsz, s)
        base, interleave, view = 0, 1, qkv
    else:
        bsz, planes, l2, n = qkv.shape
        interleave = planes // dilation
        seg = blk // interleave
        view = qkv.reshape(bsz, interleave, dilation, l2, n)
        grid = (bsz, dilation, l2 // seg)
        block = lambda width: (None, interleave, None, seg, width)
        imap = lambda col, prev: (lambda b, r, i: (b, 0, r, jnp.maximum(i - 1, 0) if prev else i, col))
        out_dims = (bsz, interleave, dilation, l2)
        base = 3 * (group - 1)
    spec = lambda which, prev=False: pl.BlockSpec(block(he), imap(base + which, prev))
    o, lse = pl.pallas_call(
        functools.partial(_attn_kernel, interleave=interleave),
        grid=grid,
        in_specs=[spec(0), spec(1), spec(1, True), spec(2), spec(2, True)],
        out_specs=[pl.BlockSpec(block(he), imap(0, False)), pl.BlockSpec(block(LSE_LANES), imap(0, False))],
        out_shape=[jax.ShapeDtypeStruct(out_dims + (he,), BF16), jax.ShapeDtypeStruct(out_dims + (LSE_LANES,), F32)],
        compiler_params=_cparams(("parallel", "parallel", "parallel")),
        name=f"dilated_attn_g{group}",
    )(view, view, view, view, view)
    if group == 0:
        return o, lse
    return o.reshape(bsz, planes, l2, he), lse.reshape(bsz, planes, l2, LSE_LANES)


def _merge_kernel(o0_ref, o1_ref, o2_ref, l0_ref, l1_ref, l2_ref, e_ref, w_ref, x_ref, mod_ref, out_ref,
                  tok_scr, lse_scr):
    planes, rows, he = o1_ref.shape
    tm = planes * rows
    l0 = jnp.concatenate(_to_planes(l0_ref[...], lse_scr, planes), axis=0)
    l1 = l1_ref[...].reshape(tm, LSE_LANES)
    l2 = l2_ref[...].reshape(tm, LSE_LANES)
    m = jnp.maximum(jnp.maximum(l0, l1), l2)
    e0, e1, e2 = jnp.exp(l0 - m), jnp.exp(l1 - m), jnp.exp(l2 - m)
    inv = 1.0 / (e0 + e1 + e2)
    expand = e_ref[...]
    part = (_split_dot(e1 * inv, expand) * o1_ref[...].reshape(tm, he).astype(F32)
            + _split_dot(e2 * inv, expand) * o2_ref[...].reshape(tm, he).astype(F32))
    w0 = _from_planes(e0 * inv, lse_scr, planes)
    o = _split_dot(w0, expand) * o0_ref[...].astype(F32) + _from_planes(part, tok_scr, planes)
    y = _dot(o.astype(BF16), w_ref[...])
    out_ref[...] = x_ref[...] + mod_ref[2:3, :] * y


def _attn_merge_proj(outs, lses, w_o, x, mod):
    bsz, s, d = x.shape
    he = outs[0].shape[-1]
    tm = ATTN_TM
    rows = tm // PLANES
    head_of_lane = jnp.arange(he, dtype=jnp.int32) // HEAD_DIM
    expand = (jnp.arange(LSE_LANES, dtype=jnp.int32)[:, None] == head_of_lane[None, :]).astype(BF16)
    tok = lambda width: pl.BlockSpec((None, tm, width), lambda b, i: (b, i, 0))
    pm = lambda width: pl.BlockSpec((None, PLANES, rows, width), lambda b, i: (b, 0, i, 0))
    return pl.pallas_call(
        _merge_kernel,
        grid=(bsz, s // tm),
        in_specs=[tok(he), pm(he), pm(he), tok(LSE_LANES), pm(LSE_LANES), pm(LSE_LANES),
                  pl.BlockSpec((LSE_LANES, he), lambda b, i: (0, 0)),
                  pl.BlockSpec((he, d), lambda b, i: (0, 0)),
                  tok(d),
                  pl.BlockSpec((None, 6, d), lambda b, i: (b, 0, 0))],
        out_specs=tok(d),
        out_shape=jax.ShapeDtypeStruct(x.shape, F32),
        scratch_shapes=[pltpu.VMEM((he // LANES, tm, LANES), F32), pltpu.VMEM((LSE_LANES // LANES, tm, LANES), F32)],
        compiler_params=_cparams(("parallel", "parallel")),
        name="attn_merge_proj",
    )(*outs, *lses, expand, w_o, x, mod)


def _attention_layer(x, mod, g, w_qkv, q_gain, k_gain, w_o, cos_t, sin_t, cos_p, sin_p):
    he = ATTN_HEADS * HEAD_DIM
    gains = jnp.stack([jnp.tile(q_gain, ATTN_HEADS), jnp.tile(k_gain, ATTN_HEADS),
                       jnp.ones((he,), F32)]).reshape(3, 1, he)
    qkv0 = _qkv_proj(x, mod, g, w_qkv[:, :3 * he], gains, cos_t, sin_t, 1)
    qkv12 = _qkv_proj(x, mod, g, w_qkv[:, 3 * he:], gains, cos_p, sin_p, PLANES)
    outs, lses = zip(_dilated_attention(qkv0, 0), _dilated_attention(qkv12, 1), _dilated_attention(qkv12, 2))
    return _attn_merge_proj(outs, lses, w_o, x, mod)


def _pw1_glu_kernel(x_ref, mod_ref, g_ref, wa_ref, wb_ref, ba_ref, bb_ref, o_ref, h_scr):
    j = pl.program_id(2)

    @pl.when(j == 0)
    def _():
        h = _normmod(x_ref[...], g_ref[...], mod_ref[1:2, :], mod_ref[0:1, :])
        h_scr[...] = h.astype(BF16)

    h = h_scr[...]
    a = _dot(h, wa_ref[...]) + ba_ref[...]
    b = _dot(h, wb_ref[...]) + bb_ref[...]
    o_ref[...] = a * _sigmoid(b)


def _pw1_glu(x, mod, g, w_pw1, b_pw1):
    bsz, s, d = x.shape
    ch = w_pw1.shape[1] // 2
    tm, tn = 1024, 512
    nj = ch // tn
    b2 = b_pw1.reshape(1, 2 * ch)
    return pl.pallas_call(
        _pw1_glu_kernel,
        grid=(bsz, s // tm, nj),
        in_specs=[
            pl.BlockSpec((None, tm, d), lambda b, i, j: (b, i, 0)),
            pl.BlockSpec((None, 6, d), lambda b, i, j: (b, 0, 0)),
            pl.BlockSpec((1, d), lambda b, i, j: (0, 0)),
            pl.BlockSpec((d, tn), lambda b, i, j: (0, j)),
            pl.BlockSpec((d, tn), lambda b, i, j: (0, j + nj)),
            pl.BlockSpec((1, tn), lambda b, i, j: (0, j)),
            pl.BlockSpec((1, tn), lambda b, i, j: (0, j + nj)),
        ],
        out_specs=pl.BlockSpec((None, tm, tn), lambda b, i, j: (b, i, j)),
        out_shape=jax.ShapeDtypeStruct((bsz, s, ch), F32),
        scratch_shapes=[pltpu.VMEM((tm, d), BF16)],
        compiler_params=_cparams(("parallel", "parallel", "arbitrary")),
        name="conv_pw1_glu",
    )(x, mod, g.reshape(1, d), w_pw1, w_pw1, b2, b2)


CONV_HALO = 32
CONV_ROWS = 16


def _dwconv_kernel(u_ref, halo_ref, wdw_ref, bdw_ref, lng_ref, lnb_ref, w2_ref, b2_ref, x_ref, mod_ref,
                   o_ref, ubuf, cbuf):
    i = pl.program_id(1)
    tm = u_ref.shape[0]
    halo = halo_ref[...]
    ubuf[0:CONV_HALO, :] = jnp.where(i > 0, halo, jnp.zeros_like(halo))
    ubuf[CONV_HALO:, :] = u_ref[...]
    first = CONV_HALO - (CONV_WIDTH - 1)
    for c in range(tm // CONV_ROWS):
        base = c * CONV_ROWS + first
        acc = jnp.zeros((CONV_ROWS, u_ref.shape[1]), F32)
        for k in range(CONV_WIDTH):
            acc = acc + wdw_ref[k:k + 1, :] * ubuf[base + k:base + k + CONV_ROWS, :]
        cbuf[c * CONV_ROWS:(c + 1) * CONV_ROWS, :] = acc
    u = cbuf[...] + bdw_ref[...]
    mu = jnp.mean(u, axis=-1, keepdims=True)
    uc = u - mu
    var = jnp.mean(uc * uc, axis=-1, keepdims=True)
    v = uc * lax.rsqrt(var + NORM_EPS) * lng_ref[...] + lnb_ref[...]
    z = (v * _sigmoid(v)).astype(BF16)
    y = _dot(z, w2_ref[...]) + b2_ref[...]
    o_ref[...] = x_ref[...] + mod_ref[2:3, :] * y


def _dwconv_ln_pw2(u, w_dw, b_dw, ln_g, ln_b, w_pw2, b_pw2, x, mod):
    bsz, s, d = x.shape
    ch = u.shape[-1]
    tm = 256
    ratio = tm // CONV_HALO
    wpad = jnp.zeros((32, ch), F32).at[:CONV_WIDTH].set(w_dw)
    row = lambda a: a.reshape(1, -1)
    return pl.pallas_call(
        _dwconv_kernel,
        grid=(bsz, s // tm),
        in_specs=[
            pl.BlockSpec((None, tm, ch), lambda b, i: (b, i, 0)),
            pl.BlockSpec((None, CONV_HALO, ch), lambda b, i: (b, jnp.maximum(i * ratio - 1, 0), 0)),
            pl.BlockSpec((32, ch), lambda b, i: (0, 0)),
            pl.BlockSpec((1, ch), lambda b, i: (0, 0)),
            pl.BlockSpec((1, ch), lambda b, i: (0, 0)),
            pl.BlockSpec((1, ch), lambda b, i: (0, 0)),
            pl.BlockSpec((ch, d), lambda b, i: (0, 0)),
            pl.BlockSpec((1, d), lambda b, i: (0, 0)),
            pl.BlockSpec((None, tm, d), lambda b, i: (b, i, 0)),
            pl.BlockSpec((None, 6, d), lambda b, i: (b, 0, 0)),
        ],
        out_specs=pl.BlockSpec((None, tm, d), lambda b, i: (b, i, 0)),
        out_shape=jax.ShapeDtypeStruct(x.shape, F32),
        scratch_shapes=[pltpu.VMEM((tm + CONV_HALO, ch), F32), pltpu.VMEM((tm, ch), F32)],
        compiler_params=_cparams(("parallel", "parallel")),
        name="conv_dw_ln_pw2",
    )(u, u, wpad, row(b_dw), row(ln_g), row(ln_b), w_pw2, row(b_pw2), x, mod)


def _conv_layer(x, mod, g, w_pw1, b_pw1, w_dw, b_dw, ln_g, ln_b, w_pw2, b_pw2):
    u = _pw1_glu(x, mod, g, w_pw1, b_pw1)
    return _dwconv_ln_pw2(u, w_dw, b_dw, ln_g, ln_b, w_pw2, b_pw2, x, mod)


def _rwkv_proj_kernel(x_ref, halo_ref, mod_ref, g_ref, mu_ref, wr_ref, wk_ref, wv_ref,
                      wa_ref, wb_ref, aa_ref, ab_ref, ga_ref, gb_ref, vec_ref,
                      r_ref, ld_ref, k_ref, v_ref, kk_ref, a_ref, gate_ref):
    i = pl.program_id(1)
    g = g_ref[...]
    scale, shift = mod_ref[1:2, :], mod_ref[0:1, :]
    h = _normmod(x_ref[...], g, scale, shift)
    h_halo = _normmod(halo_ref[...], g, scale, shift)
    last = jnp.where(i > 0, h_halo[7:8, :], jnp.zeros_like(h_halo[7:8, :]))
    rows = lax.broadcasted_iota(jnp.int32, h.shape, 0)
    h_prev = jnp.where(rows == 0, last, pltpu.roll(h, 1, 0))
    xx = h_prev - h

    def mix(m):
        return (h + xx * mu_ref[m:m + 1, :]).astype(BF16)

    w0, a0, k_k, k_a = vec_ref[0:1, :], vec_ref[1:2, :], vec_ref[2:3, :], vec_ref[3:4, :]
    r = _dot(mix(0), wr_ref[...])
    k = _dot(mix(2), wk_ref[...])
    v = _dot(mix(3), wv_ref[...])
    zw = w0 + _dot(jnp.tanh(_dot(mix(1), wa_ref[...])).astype(BF16), wb_ref[...])
    w = -(jnp.maximum(-zw, 0.0) + jnp.log1p(jnp.exp(-jnp.abs(zw)))) - 0.5
    a = _sigmoid(a0 + _dot(_dot(mix(4), aa_ref[...]).astype(BF16), ab_ref[...]))
    gate = _dot(_sigmoid(_dot(mix(5), ga_ref[...])).astype(BF16), gb_ref[...])
    kk = k * k_k
    ones = _head_ones(LANE_GROUP)
    sq = (kk * kk).astype(BF16)
    d = kk.shape[1]
    ss = jnp.concatenate([_dot(sq[:, c:c + LANE_GROUP], ones) for c in range(0, d, LANE_GROUP)], axis=1)
    kk = kk / jnp.maximum(jnp.sqrt(ss), 1e-12)
    r_ref[...] = r
    ld_ref[...] = -jnp.exp(w)
    k_ref[...] = k * (1.0 + (a - 1.0) * k_a)
    v_ref[...] = v
    kk_ref[...] = kk
    a_ref[...] = a
    gate_ref[...] = gate


def _rwkv_proj(x, mod, g, mu, w_r, w_k, w_v, wa, wb, aa, ab, ga, gb, vecs):
    bsz, s, d = x.shape
    tm = 256
    ratio = tm // 8
    full = lambda a: pl.BlockSpec(a.shape, lambda b, i: (0,) * a.ndim)
    tok = pl.BlockSpec((None, tm, d), lambda b, i: (b, i, 0))
    mu8 = jnp.zeros((8, d), F32).at[:6].set(mu)
    weights = [w_r, w_k, w_v, wa, wb, aa, ab, ga, gb, vecs]
    return pl.pallas_call(
        _rwkv_proj_kernel,
        grid=(bsz, s // tm),
        in_specs=[tok,
                  pl.BlockSpec((None, 8, d), lambda b, i: (b, jnp.maximum(i * ratio - 1, 0), 0)),
                  pl.BlockSpec((None, 6, d), lambda b, i: (b, 0, 0)),
                  pl.BlockSpec((1, d), lambda b, i: (0, 0)),
                  full(mu8)] + [full(a) for a in weights],
        out_specs=[tok] * 7,
        out_shape=[jax.ShapeDtypeStruct(x.shape, F32)] * 7,
        compiler_params=_cparams(("parallel", "parallel")),
        name="rwkv_proj",
    )(x, x, mod, g.reshape(1, d), mu8, *weights)


def _wkv_kernel(r_ref, ld_ref, k_ref, v_ref, kk_ref, a_ref, gate_ref, vec_ref, o_ref, state):
    t = pl.program_id(1)

    @pl.when(t == 0)
    def _():
        state[...] = jnp.zeros_like(state)

    c = CHUNK
    w = LANE_GROUP
    nh = w // HEAD_DIM
    n_streams = r_ref.shape[0]
    n_chunks = r_ref.shape[1] // c
    n_groups = r_ref.shape[2] // w
    lane_sl = [slice(g * w, (g + 1) * w) for g in range(n_groups)]
    assert c == HEAD_DIM

    lane_head = _head_of((c, w), 1)
    head_masks = [lane_head == h for h in range(nh)]
    row_c = lax.broadcasted_iota(jnp.int32, (c, w), 0)
    col_in = _pos_in_head((c, w), 1)
    strict = col_in < row_c
    incl = col_in <= row_c
    eye = (col_in == row_c).astype(F32)
    tri = (lax.broadcasted_iota(jnp.int32, (c, c), 1) <= lax.broadcasted_iota(jnp.int32, (c, c), 0)).astype(BF16)
    diag_blocks = _head_of((w, w), 0) == _head_of((w, w), 1)
    ones = diag_blocks.astype(BF16)
    zero16 = jnp.zeros((), BF16)

    def bd(m):
        m16 = m.astype(BF16)
        return jnp.concatenate([jnp.where(msk, m16, zero16) for msk in head_masks], axis=0)

    def bd_cols(m):
        m16 = m.astype(BF16)
        return jnp.where(diag_blocks, jnp.concatenate([m16] * nh, axis=0), zero16)

    def mm(a, b):
        return _dot(a.astype(BF16), b.astype(BF16))

    def prefix_sum(x):
        h1 = x.astype(BF16)
        r1 = x - h1.astype(F32)
        h2 = r1.astype(BF16)
        h3 = (r1 - h2.astype(F32)).astype(BF16)
        return _dot(tri, h1) + _dot(tri, h2) + _dot(tri, h3)

    def each(fn, *lists):
        return [fn(*args) for args in zip(*lists)]

    def body(ci, carry):
        rows = pl.ds(pl.multiple_of(ci * c, c), c)
        units = [(s, lane_sl[g]) for s in range(n_streams) for g in range(n_groups)]
        ld = [ld_ref[s, rows, ln] for s, ln in units]
        kk = [kk_ref[s, rows, ln] for s, ln in units]
        v = [v_ref[s, rows, ln] for s, ln in units]

        h1 = each(lambda x: x.astype(BF16), ld)
        r1 = each(lambda x, h: x - h.astype(F32), ld, h1)
        h2 = each(lambda x: x.astype(BF16), r1)
        h3 = each(lambda x, h: (x - h.astype(F32)).astype(BF16), r1, h2)
        c1 = each(lambda h: _dot(tri, h), h1)
        c2 = each(lambda h: _dot(tri, h), h2)
        c3 = each(lambda h: _dot(tri, h), h3)
        cum = each(lambda x, y, z: x + y + z, c1, c2, c3)
        p_in = each(jnp.exp, cum)
        p_inv = each(lambda x: jnp.exp(-x), cum)
        p_end = each(lambda p: p[c - 1:c, :], p_in)
        at = each(lambda q, x, l: -q * jnp.exp(x - l), kk, cum, ld)
        bt = [q * a_ref[s, rows, ln] * pi for (s, ln), q, pi in zip(units, kk, p_inv)]
        kt = [k_ref[s, rows, ln] * pi for (s, ln), pi in zip(units, p_inv)]
        lhs = [jnp.concatenate([x, r_ref[s, rows, ln] * p], axis=0).astype(BF16)
               for (s, ln), x, p in zip(units, at, p_in)]
        bt_bd = each(bd, bt)
        kt_bd = each(bd, kt)
        g_b = each(_dot_t, lhs, bt_bd)
        g_k = each(_dot_t, lhs, kt_bd)
        a_ab = each(lambda g: jnp.where(strict, g[:c], 0.0), g_b)
        a_ak = each(lambda g: jnp.where(strict, g[:c], 0.0).astype(BF16), g_k)
        b_cat = each(lambda gb, gk: jnp.concatenate(
            [jnp.where(incl, gb[c:], 0.0), jnp.where(incl, gk[c:], 0.0)], axis=1).astype(BF16), g_b, g_k)

        x_inv = each(lambda m: eye + m, a_ab)
        pw = each(lambda m: mm(m, bd_cols(m)), a_ab)
        for _ in range(4):
            both = each(lambda x, p: mm(jnp.concatenate([x, p], axis=0), bd_cols(p)), x_inv, pw)
            x_inv = each(lambda x, b: x + b[:c], x_inv, both)
            pw = each(lambda b: b[c:], both)
        x_inv = each(lambda x, p: (x + mm(x, bd_cols(p))).astype(BF16), x_inv, pw)

        v_bd = each(bd, v)
        akv = each(_dot, a_ak, v_bd)
        bk = each(lambda b, k, p: jnp.concatenate([b * p, k * p], axis=0).astype(BF16), bt, kt, p_end)
        rkr = [(r_ref[s, rows, ln] * k_ref[s, rows, ln] * vec_ref[0:1, ln]).astype(BF16) for s, ln in units]
        bonus = each(lambda x, val: _dot(x, ones) * val, rkr, v)

        s_bd = [state[s, g] for s in range(n_streams) for g in range(n_groups)]
        sa_sr = each(lambda l, st: _dot_t(l, st.astype(BF16)), lhs, s_bd)
        u_rhs = each(lambda x, y: bd(x[:c] + y), sa_sr, akv)
        u = each(_dot, x_inv, u_rhs)
        uv_t = each(lambda x, val: jnp.concatenate([x, val], axis=0).T.astype(BF16), u, v)
        upd = each(_dot, uv_t, bk)
        s_new = each(lambda st, p, x: st * p + jnp.where(diag_blocks, x, 0.0), s_bd, p_end, upd)
        i = 0
        for s in range(n_streams):
            for g in range(n_groups):
                state[s, g] = s_new[i]
                i += 1
        y_rhs = each(lambda x, vb: jnp.concatenate([bd(x), vb], axis=0), u, v_bd)
        y2 = each(_dot, b_cat, y_rhs)
        y = each(lambda x, z: x[c:] + z, sa_sr, y2)

        inv_n = 1.0 / HEAD_DIM
        y_hi = each(lambda x: x.astype(BF16), y)
        y_lo = each(lambda x, h: (x - h.astype(F32)).astype(BF16), y, y_hi)
        m_hi = each(lambda h: _dot(h, ones), y_hi)
        m_lo = each(lambda h: _dot(h, ones), y_lo)
        yc = each(lambda x, a, b: x - (a + b) * inv_n, y, m_hi, m_lo)
        var = each(lambda x: _dot((x * x).astype(BF16), ones) * inv_n, yc)
        for (s, ln), x, vr, bo in zip(units, yc, var, bonus):
            yn = x * lax.rsqrt(vr + RWKV_GN_EPS) * vec_ref[1:2, ln] + vec_ref[2:3, ln]
            o_ref[s, rows, ln] = ((yn + bo) * gate_ref[s, rows, ln]).astype(o_ref.dtype)
        return carry

    lax.fori_loop(0, n_chunks, body, 0)


WKV_TOKENS = 256


def _wkv_scan(r, ld, k, v, kk, a, gate, vecs):
    bsz, s, d = r.shape
    tb, lw = WKV_TOKENS, d
    tok = pl.BlockSpec((bsz, tb, lw), lambda gi, t: (0, t, gi))
    return pl.pallas_call(
        _wkv_kernel,
        grid=(d // lw, s // tb),
        in_specs=[tok] * 7 + [pl.BlockSpec((8, lw), lambda gi, t: (0, gi))],
        out_specs=tok,
        out_shape=jax.ShapeDtypeStruct((bsz, s, d), BF16),
        scratch_shapes=[pltpu.VMEM((bsz, lw // LANE_GROUP, LANE_GROUP, LANE_GROUP), F32)],
        compiler_params=_cparams(("parallel", "arbitrary")),
        name="wkv7_scan",
    )(r, ld, k, v, kk, a, gate, vecs)


def _rwkv_layer(x, mod, g, mu, w_r, w_k, w_v, w_o, w0, wa, wb, a0, aa, ab, ga, gb, k_k, k_a, r_k, ln_g, ln_b):
    d = x.shape[-1]
    vecs = jnp.zeros((8, d), F32).at[0].set(w0).at[1].set(a0).at[2].set(k_k).at[3].set(k_a)
    r, ld, k, v, kk, a, gate = _rwkv_proj(x, mod, g, mu, w_r, w_k, w_v, wa, wb, aa, ab, ga, gb, vecs)
    svecs = jnp.zeros((8, d), F32).at[0].set(r_k.reshape(d)).at[1].set(ln_g).at[2].set(ln_b)
    z = _wkv_scan(r, ld, k, v, kk, a, gate, svecs)
    return _proj_residual(z, w_o, jnp.zeros((d,), F32), x, mod)


def _rope_tables(positions):
    half = HEAD_DIM // 2
    inv_freq = jnp.power(ROPE_THETA, -jnp.arange(0, HEAD_DIM, 2, dtype=F32) / HEAD_DIM)
    ang = positions.astype(F32)[..., None] * inv_freq
    cos, sin = jnp.cos(ang), jnp.sin(ang)
    assert cos.shape[-1] == half
    return jnp.concatenate([cos, cos, cos, cos], -1), jnp.concatenate([-sin, sin, -sin, sin], -1)


def _plane_major(t):
    bsz, s, n = t.shape
    return t.reshape(bsz, s // PLANES, PLANES, n).transpose(0, 2, 1, 3)


def kernel(x, c, positions, ada_w, ada_b, norm_mix_g, norm_ffn_g, ffn_w_gate, ffn_w_up, ffn_w_down, attn_w_qkv, attn_q_gain, attn_k_gain, attn_w_o, conv_w_pw1, conv_b_pw1, conv_w_dw, conv_b_dw, conv_ln_g, conv_ln_b, conv_w_pw2, conv_b_pw2, rwkv_mu, rwkv_w_r, rwkv_w_k, rwkv_w_v, rwkv_w_o, rwkv_w0, rwkv_w_lora_a, rwkv_w_lora_b, rwkv_a0, rwkv_a_lora_a, rwkv_a_lora_b, rwkv_g_lora_a, rwkv_g_lora_b, rwkv_k_k, rwkv_k_a, rwkv_r_k, rwkv_ln_g, rwkv_ln_b):
    depth = ada_w.shape[0]
    bf = lambda a: a.astype(BF16)
    cos_t, sin_t = _rope_tables(positions)
    cos_p, sin_p = _plane_major(cos_t), _plane_major(sin_t)
    mods = _ada_mod(c, ada_w, ada_b)
    for i in range(depth):
        mod = mods[i]
        kind, j = i % 3, i // 3
        if kind == 0:
            x = _attention_layer(x, mod, norm_mix_g[i], bf(attn_w_qkv[j]), attn_q_gain[j], attn_k_gain[j],
                                 bf(attn_w_o[j]), cos_t, sin_t, cos_p, sin_p)
        elif kind == 1:
            x = _conv_layer(x, mod, norm_mix_g[i], bf(conv_w_pw1[j]), conv_b_pw1[j], conv_w_dw[j], conv_b_dw[j],
                            conv_ln_g[j], conv_ln_b[j], bf(conv_w_pw2[j]), conv_b_pw2[j])
        else:
            x = _rwkv_layer(x, mod, norm_mix_g[i], rwkv_mu[j], bf(rwkv_w_r[j]), bf(rwkv_w_k[j]), bf(rwkv_w_v[j]),
                            bf(rwkv_w_o[j]), rwkv_w0[j], bf(rwkv_w_lora_a[j]), bf(rwkv_w_lora_b[j]), rwkv_a0[j],
                            bf(rwkv_a_lora_a[j]), bf(rwkv_a_lora_b[j]), bf(rwkv_g_lora_a[j]), bf(rwkv_g_lora_b[j]),
                            rwkv_k_k[j], rwkv_k_a[j], rwkv_r_k[j], rwkv_ln_g[j], rwkv_ln_b[j])
        x = _ffn(x, mod, norm_ffn_g[i], bf(ffn_w_gate[i]), bf(ffn_w_up[i]), bf(ffn_w_down[i]))
    return x
```

```python
import functools

import jax
import jax.numpy as jnp
from jax import lax
from jax.experimental import pallas as pl
from jax.experimental.pallas import tpu as pltpu

F32 = jnp.float32
BF16 = jnp.bfloat16

NORM_EPS = 1e-6
HEAD_DIM = 64
ATTN_HEADS = 16
DILATED_GROUPS = ((128, 1), (512, 4), (2048, 16))
ATTN_BLOCK = 128
ROPE_THETA = 10000.0
NEG_INF = -1e30
CONV_WIDTH = 31
RWKV_GN_EPS = 1e-5 * HEAD_DIM
CHUNK = 64
LANE_GROUP = 256
VMEM_LIMIT = 56 * 1024 * 1024


def _cparams(sem):
    return pltpu.CompilerParams(dimension_semantics=sem, vmem_limit_bytes=VMEM_LIMIT)


def _dot(a, b):
    return jnp.dot(a, b, preferred_element_type=F32)


def _dot_t(a, b):
    return lax.dot_general(a, b, (((1,), (1,)), ((), ())), preferred_element_type=F32)


def _split_dot(x, w):
    hi = x.astype(BF16)
    lo = (x - hi.astype(F32)).astype(BF16)
    return _dot(hi, w) + _dot(lo, w)


def _sigmoid(x):
    return 1.0 / (1.0 + jnp.exp(-x))


def _normmod(x, g, scale, shift):
    ms = jnp.mean(x * x, axis=-1, keepdims=True)
    return x * lax.rsqrt(ms + NORM_EPS) * g * (1.0 + scale) + shift


HEAD_SHIFT = HEAD_DIM.bit_length() - 1
assert 1 << HEAD_SHIFT == HEAD_DIM


def _head_of(shape, axis):
    return jnp.right_shift(lax.broadcasted_iota(jnp.int32, shape, axis), HEAD_SHIFT)


def _pos_in_head(shape, axis):
    return jnp.bitwise_and(lax.broadcasted_iota(jnp.int32, shape, axis), HEAD_DIM - 1)


def _head_ones(n):
    return (_head_of((n, n), 0) == _head_of((n, n), 1)).astype(BF16)


def _ada_kernel(c_ref, w_ref, b_ref, o_ref):
    c = c_ref[...]
    c_act = (c * _sigmoid(c)).astype(BF16)
    o_ref[...] = _dot(c_act, w_ref[...].astype(BF16)) + b_ref[...]


def _ada_mod(c, ada_w, ada_b):
    depth, d, n = ada_w.shape
    bsz = c.shape[0]
    rows = 8
    c_pad = jnp.zeros((rows, d), F32).at[:bsz].set(c)
    tn = 1536
    out = pl.pallas_call(
        _ada_kernel,
        grid=(depth, n // tn),
        in_specs=[
            pl.BlockSpec((rows, d), lambda i, j: (0, 0)),
            pl.BlockSpec((None, d, tn), lambda i, j: (i, 0, j)),
            pl.BlockSpec((None, 1, tn), lambda i, j: (i, 0, j)),
        ],
        out_specs=pl.BlockSpec((None, rows, tn), lambda i, j: (i, 0, j)),
        out_shape=jax.ShapeDtypeStruct((depth, rows, n), F32),
        compiler_params=_cparams(("parallel", "parallel")),
        name="ada_mod",
    )(c_pad, ada_w, ada_b.reshape(depth, 1, n))
    return out[:, :bsz].reshape(depth, bsz, 6, d)


FFN_SUB = 256


def _ffn_kernel(x_ref, mod_ref, g_ref, wg_ref, wu_ref, wd_ref, o_ref, h_scr, acc_scr):
    j = pl.program_id(2)

    @pl.when(j == 0)
    def _():
        h = _normmod(x_ref[...], g_ref[...], mod_ref[4:5, :], mod_ref[3:4, :])
        h_scr[...] = h.astype(BF16)
        acc_scr[...] = jnp.zeros_like(acc_scr)

    h = h_scr[...]
    tf = wg_ref.shape[1]
    cols = [(c0, min(c0 + FFN_SUB, tf)) for c0 in range(0, tf, FFN_SUB)]
    gate_up = lambda c: (_dot(h, wg_ref[:, c[0]:c[1]]), _dot(h, wu_ref[:, c[0]:c[1]]))
    nxt = gate_up(cols[0])
    for idx, c in enumerate(cols):
        a, u = nxt
        if idx + 1 < len(cols):
            nxt = gate_up(cols[idx + 1])
        z = (a * _sigmoid(a) * u).astype(BF16)
        acc_scr[...] += _dot(z, wd_ref[c[0]:c[1], :])

    @pl.when(j == pl.num_programs(2) - 1)
    def _():
        o_ref[...] = x_ref[...] + mod_ref[5:6, :] * acc_scr[...]


def _ffn(x, mod, g, wg, wu, wd):
    bsz, s, d = x.shape
    f = wg.shape[1]
    tm, tf = 1024, f // 2
    return pl.pallas_call(
        _ffn_kernel,
        grid=(bsz, s // tm, f // tf),
        in_specs=[
            pl.BlockSpec((None, tm, d), lambda b, i, j: (b, i, 0)),
            pl.BlockSpec((None, 6, d), lambda b, i, j: (b, 0, 0)),
            pl.BlockSpec((1, d), lambda b, i, j: (0, 0)),
            pl.BlockSpec((d, tf), lambda b, i, j: (0, j)),
            pl.BlockSpec((d, tf), lambda b, i, j: (0, j)),
            pl.BlockSpec((tf, d), lambda b, i, j: (j, 0)),
        ],
        out_specs=pl.BlockSpec((None, tm, d), lambda b, i, j: (b, i, 0)),
        out_shape=jax.ShapeDtypeStruct(x.shape, F32),
        scratch_shapes=[pltpu.VMEM((tm, d), BF16), pltpu.VMEM((tm, d), F32)],
        compiler_params=_cparams(("parallel", "parallel", "arbitrary")),
        name="ffn",
    )(x, mod, g.reshape(1, d), wg, wu, wd)


def _proj_res_kernel(z_ref, w_ref, b_ref, x_ref, mod_ref, o_ref):
    y = _dot(z_ref[...], w_ref[...]) + b_ref[...]
    o_ref[...] = x_ref[...] + mod_ref[2:3, :] * y


def _proj_residual(z, w, bias, x, mod):
    bsz, s, d = x.shape
    k = z.shape[-1]
    tm = 1024
    return pl.pallas_call(
        _proj_res_kernel,
        grid=(bsz, s // tm),
        in_specs=[
            pl.BlockSpec((None, tm, k), lambda b, i: (b, i, 0)),
            pl.BlockSpec((k, d), lambda b, i: (0, 0)),
            pl.BlockSpec((1, d), lambda b, i: (0, 0)),
            pl.BlockSpec((None, tm, d), lambda b, i: (b, i, 0)),
            pl.BlockSpec((None, 6, d), lambda b, i: (b, 0, 0)),
        ],
        out_specs=pl.BlockSpec((None, tm, d), lambda b, i: (b, i, 0)),
        out_shape=jax.ShapeDtypeStruct(x.shape, F32),
        compiler_params=_cparams(("parallel", "parallel")),
        name="proj_residual",
    )(z, w, bias.reshape(1, d), x, mod)


PLANES = DILATED_GROUPS[-1][1]
ATTN_TM = 1024


LANES = 128


def _to_planes(val, scr, planes):
    tm, n = val.shape
    rows = tm // planes
    for c in range(n // LANES):
        scr[c] = val[:, c * LANES:(c + 1) * LANES]
    return [jnp.concatenate([scr[c, pl.ds(p, rows, stride=planes), :] for c in range(n // LANES)], axis=1)
            for p in range(planes)]


def _from_planes(val, scr, planes):
    tm, n = val.shape
    rows = tm // planes
    for p in range(planes):
        for c in range(n // LANES):
            scr[c, pl.ds(p, rows, stride=planes), :] = val[p * rows:(p + 1) * rows, c * LANES:(c + 1) * LANES]
    return jnp.concatenate([scr[c] for c in range(n // LANES)], axis=1)


QK_ROWS = 32


def _block_rows_ref(ref, r0, nrows):
    if len(ref.shape) == 2:
        return ref.at[r0:r0 + nrows, :]
    per = ref.shape[1]
    return ref.at[r0 // per, r0 % per:r0 % per + nrows, :]


def _rms_rope_store(y_scr, ss_scr, gain_ref, cos_ref, sin_ref, o_ref):
    tm, n = y_scr.shape
    tw = cos_ref.shape[-1]
    half = HEAD_DIM // 2
    gain = gain_ref[:, 0:tw]
    gain_rot = pltpu.roll(gain, half, 1)
    lane_lo = _pos_in_head((QK_ROWS, n), 1) < half
    for r0 in range(0, tm, QK_ROWS):
        y = y_scr[r0:r0 + QK_ROWS, :]
        r = lax.rsqrt(ss_scr[r0:r0 + QK_ROWS, :] * (1.0 / HEAD_DIM) + NORM_EPS)
        cos = _block_rows_ref(cos_ref, r0, QK_ROWS)[...] * gain
        sin = _block_rows_ref(sin_ref, r0, QK_ROWS)[...] * gain_rot
        cos = jnp.concatenate([cos] * (n // tw), axis=1)
        sin = jnp.concatenate([sin] * (n // tw), axis=1)
        partner = jnp.where(lane_lo, pltpu.roll(y, n - half, 1), pltpu.roll(y, half, 1))
        _block_rows_ref(o_ref, r0, QK_ROWS)[...] = (r * (y * cos + partner * sin)).astype(o_ref.dtype)


def _qkv_kernel(x_ref, mod_ref, g_ref, w_ref, gain_ref, cos_ref, sin_ref, o_ref, h_scr, hf_scr, y_scr, ss_scr,
                *, planes):
    j = pl.program_id(2)
    tm = x_ref.shape[0]

    @pl.when(j == 0)
    def _():
        h = _normmod(x_ref[...], g_ref[...], mod_ref[1:2, :], mod_ref[0:1, :])
        if planes == 1:
            h_scr[...] = h.astype(BF16)
        else:
            rows = tm // planes
            for p, hp in enumerate(_to_planes(h, hf_scr, planes)):
                h_scr[p * rows:(p + 1) * rows, :] = hp.astype(BF16)

    y = _dot(h_scr[...], w_ref[...])
    kind = j % 3

    @pl.when(kind == 2)
    def _():
        o_ref[...] = y.astype(o_ref.dtype).reshape(o_ref.shape)

    @pl.when(kind != 2)
    def _():
        n = y.shape[1]
        ones = _head_ones(LANE_GROUP)
        y_scr[...] = y
        sq = (y * y).astype(BF16)
        for c in range(0, n, LANE_GROUP):
            ss_scr[:, c:c + LANE_GROUP] = _dot(sq[:, c:c + LANE_GROUP], ones)
        _rms_rope_store(y_scr, ss_scr, gain_ref, cos_ref, sin_ref, o_ref)


def _qkv_proj(x, mod, g, w, gains, cos_t, sin_t, planes):
    bsz, s, d = x.shape
    n = w.shape[1]
    tn = ATTN_HEADS * HEAD_DIM
    tm = ATTN_TM
    tw = cos_t.shape[-1]
    if planes == 1:
        tok = lambda width: pl.BlockSpec((None, tm, width), lambda b, i, j: (b, i, 0))
        out_spec = pl.BlockSpec((None, tm, tn), lambda b, i, j: (b, i, j))
        out_shape = jax.ShapeDtypeStruct((bsz, s, n), BF16)
    else:
        rows = tm // planes
        tok = lambda width: pl.BlockSpec((None, planes, rows, width), lambda b, i, j: (b, 0, i, 0))
        out_spec = pl.BlockSpec((None, planes, rows, tn), lambda b, i, j: (b, 0, i, j))
        out_shape = jax.ShapeDtypeStruct((bsz, planes, s // planes, n), BF16)
    return pl.pallas_call(
        functools.partial(_qkv_kernel, planes=planes),
        grid=(bsz, s // tm, n // tn),
        in_specs=[
            pl.BlockSpec((None, tm, d), lambda b, i, j: (b, i, 0)),
            pl.BlockSpec((None, 6, d), lambda b, i, j: (b, 0, 0)),
            pl.BlockSpec((1, d), lambda b, i, j: (0, 0)),
            pl.BlockSpec((d, tn), lambda b, i, j: (0, j)),
            pl.BlockSpec((None, 1, tn), lambda b, i, j: (j % 3, 0, 0)),
            tok(tw), tok(tw),
        ],
        out_specs=out_spec,
        out_shape=out_shape,
        scratch_shapes=[pltpu.VMEM((tm, d), BF16), pltpu.VMEM((d // LANES, tm, LANES), F32),
                        pltpu.VMEM((tm, tn), F32), pltpu.VMEM((tm, tn), F32)],
        compiler_params=_cparams(("parallel", "parallel", "arbitrary")),
        name=f"attn_qkv_p{planes}",
    )(x, mod, g.reshape(1, d), w, gains, cos_t, sin_t)


ATTN_STEP_BLOCKS = 2


def _attn_kernel(q_ref, k_ref, v_ref, kp_ref, vp_ref, o_ref, lse_ref, *, interleave):
    n = pl.program_id(2)
    blk = ATTN_BLOCK
    he = ATTN_HEADS * HEAD_DIM
    seg = blk // interleave

    def pos(axis):
        i = lax.broadcasted_iota(jnp.int32, (blk, blk), axis)
        if interleave == 1:
            return i
        return interleave * jnp.bitwise_and(i, seg - 1) + jnp.right_shift(i, seg.bit_length() - 1)

    def block_of(ref, b):
        if len(ref.shape) == 2:
            return ref[b * blk:(b + 1) * blk, :]
        return ref[:, b * seg:(b + 1) * seg, :].reshape(blk, he)

    qpos, kpos = pos(0), pos(1)
    cur_ok = kpos <= qpos
    prev_ok = kpos >= qpos
    first_ok = kpos >= qpos + jnp.where(n > 0, 0, blk)
    pair = 2 * HEAD_DIM
    lane = lax.broadcasted_iota(jnp.int32, (blk, pair), 1)
    first = lane < HEAD_DIM
    zero = jnp.zeros((), BF16)
    scale = jnp.asarray(HEAD_DIM ** -0.5, BF16)
    sl = lambda hp: slice(hp * pair, (hp + 1) * pair)
    n_blocks = ATTN_STEP_BLOCKS

    q = [block_of(q_ref, b) * scale for b in range(n_blocks)]
    kc = [block_of(k_ref, b) for b in range(n_blocks)]
    vc = [block_of(v_ref, b) for b in range(n_blocks)]
    kp = [kp_ref[...].reshape(blk, he)] + kc[:-1]
    vp = [vp_ref[...].reshape(blk, he)] + vc[:-1]
    masks = [first_ok] + [prev_ok] * (n_blocks - 1)

    units = [(b, hp, sub) for b in range(n_blocks) for hp in range(ATTN_HEADS // 2) for sub in range(2)]
    qh = [jnp.where(first if sub == 0 else jnp.logical_not(first), q[b][:, sl(hp)], zero) for b, hp, sub in units]
    s_c = [jnp.where(cur_ok, _dot_t(x, kc[b][:, sl(hp)]), NEG_INF) for x, (b, hp, _) in zip(qh, units)]
    s_p = [jnp.where(masks[b], _dot_t(x, kp[b][:, sl(hp)]), NEG_INF) for x, (b, hp, _) in zip(qh, units)]
    m = [jnp.max(jnp.maximum(a, c), axis=-1, keepdims=True) for a, c in zip(s_c, s_p)]
    p_c = [jnp.exp(a - mx) for a, mx in zip(s_c, m)]
    p_p = [jnp.exp(a - mx) for a, mx in zip(s_p, m)]
    l = [jnp.sum(a + c, axis=-1, keepdims=True) for a, c in zip(p_c, p_p)]
    pv = [_dot(a.astype(BF16), vc[b][:, sl(hp)]) + _dot(c.astype(BF16), vp[b][:, sl(hp)])
          for a, c, (b, hp, _) in zip(p_c, p_p, units)]
    o = [x / dd for x, dd in zip(pv, l)]
    lse_lane = lax.broadcasted_iota(jnp.int32, (blk, lse_ref.shape[-1]), 1)
    per_block = ATTN_HEADS
    for b in range(n_blocks):
        ob = o[b * per_block:(b + 1) * per_block]
        out = jnp.concatenate([jnp.where(first, ob[2 * hp], ob[2 * hp + 1]) for hp in range(ATTN_HEADS // 2)], axis=1)
        lse_tile = jnp.zeros((blk, lse_ref.shape[-1]), F32)
        for idx in range(per_block):
            u = b * per_block + idx
            lse_tile = jnp.where(lse_lane == idx, m[u] + jnp.log(l[u]), lse_tile)
        if len(o_ref.shape) == 2:
            o_ref[b * blk:(b + 1) * blk, :] = out.astype(o_ref.dtype)
            lse_ref[b * blk:(b + 1) * blk, :] = lse_tile
        else:
            o_ref[:, b * seg:(b + 1) * seg, :] = out.astype(o_ref.dtype).reshape(interleave, seg, he)
            lse_ref[:, b * seg:(b + 1) * seg, :] = lse_tile.reshape(interleave, seg, lse_ref.shape[-1])


LSE_LANES = 128


def _dilated_attention(qkv, group):
    he = ATTN_HEADS * HEAD_DIM
    blk = ATTN_BLOCK
    nb = ATTN_STEP_BLOCKS
    dilation = DILATED_GROUPS[group][1]
    if group == 0:
        bsz, s, _ = qkv.shape
        grid = (bsz, 1, s // (nb * blk))
        block = lambda rows, width: (None, rows, width)
        imap = lambda col: (lambda b, r, i: (b, i, col))
        imap_prev = lambda col: (lambda b, r, i: (b, jnp.maximum(nb * i - 1, 0), col))
        out_dims = (bsz, s)
        base, interleave, view, seg = 0, 1, qkv, blk
    else:
        bsz, planes, l2, n = qkv.shape
        interleave = planes // dilation
        seg = blk // interleave
        view = qkv.reshape(bsz, interleave, dilation, l2, n)
        grid = (bsz, dilation, l2 // (nb * seg))
        block = lambda rows, width: (None, interleave, None, rows, width)
        imap = lambda col: (lambda b, r, i: (b, 0, r, i, col))
        imap_prev = lambda col: (lambda b, r, i: (b, 0, r, jnp.maximum(nb * i - 1, 0), col))
        out_dims = (bsz, interleave, dilation, l2)
        base = 3 * (group - 1)
    cur = lambda which: pl.BlockSpec(block(nb * seg, he), imap(base + which))
    prev = lambda which: pl.BlockSpec(block(seg, he), imap_prev(base + which))
    o, lse = pl.pallas_call(
        functools.partial(_attn_kernel, interleave=interleave),
        grid=grid,
        in_specs=[cur(0), cur(1), cur(2), prev(1), prev(2)],
        out_specs=[pl.BlockSpec(block(nb * seg, he), imap(0)), pl.BlockSpec(block(nb * seg, LSE_LANES), imap(0))],
        out_shape=[jax.ShapeDtypeStruct(out_dims + (he,), BF16), jax.ShapeDtypeStruct(out_dims + (LSE_LANES,), F32)],
        compiler_params=_cparams(("parallel", "parallel", "parallel")),
        name=f"dilated_attn_g{group}",
    )(view, view, view, view, view)
    if group == 0:
        return o, lse
    return o.reshape(bsz, planes, l2, he), lse.reshape(bsz, planes, l2, LSE_LANES)


def _merge_kernel(o0_ref, o1_ref, o2_ref, l0_ref, l1_ref, l2_ref, e_ref, w_ref, x_ref, mod_ref, out_ref,
                  tok_scr, lse_scr):
    planes, rows, he = o1_ref.shape
    tm = planes * rows
    l0 = jnp.concatenate(_to_planes(l0_ref[...], lse_scr, planes), axis=0)
    l1 = l1_ref[...].reshape(tm, LSE_LANES)
    l2 = l2_ref[...].reshape(tm, LSE_LANES)
    m = jnp.maximum(jnp.maximum(l0, l1), l2)
    e0, e1, e2 = jnp.exp(l0 - m), jnp.exp(l1 - m), jnp.exp(l2 - m)
    inv = 1.0 / (e0 + e1 + e2)
    expand = e_ref[...]
    part = (_split_dot(e1 * inv, expand) * o1_ref[...].reshape(tm, he).astype(F32)
            + _split_dot(e2 * inv, expand) * o2_ref[...].reshape(tm, he).astype(F32))
    w0 = _from_planes(e0 * inv, lse_scr, planes)
    o = _split_dot(w0, expand) * o0_ref[...].astype(F32) + _from_planes(part, tok_scr, planes)
    y = _dot(o.astype(BF16), w_ref[...])
    out_ref[...] = x_ref[...] + mod_ref[2:3, :] * y


def _attn_merge_proj(outs, lses, w_o, x, mod):
    bsz, s, d = x.shape
    he = outs[0].shape[-1]
    tm = ATTN_TM
    rows = tm // PLANES
    head_of_lane = jnp.arange(he, dtype=jnp.int32) // HEAD_DIM
    expand = (jnp.arange(LSE_LANES, dtype=jnp.int32)[:, None] == head_of_lane[None, :]).astype(BF16)
    tok = lambda width: pl.BlockSpec((None, tm, width), lambda b, i: (b, i, 0))
    pm = lambda width: pl.BlockSpec((None, PLANES, rows, width), lambda b, i: (b, 0, i, 0))
    return pl.pallas_call(
        _merge_kernel,
        grid=(bsz, s // tm),
        in_specs=[tok(he), pm(he), pm(he), tok(LSE_LANES), pm(LSE_LANES), pm(LSE_LANES),
                  pl.BlockSpec((LSE_LANES, he), lambda b, i: (0, 0)),
                  pl.BlockSpec((he, d), lambda b, i: (0, 0)),
                  tok(d),
                  pl.BlockSpec((None, 6, d), lambda b, i: (b, 0, 0))],
        out_specs=tok(d),
        out_shape=jax.ShapeDtypeStruct(x.shape, F32),
        scratch_shapes=[pltpu.VMEM((he // LANES, tm, LANES), F32), pltpu.VMEM((LSE_LANES // LANES, tm, LANES), F32)],
        compiler_params=_cparams(("parallel", "parallel")),
        name="attn_merge_proj",
    )(*outs, *lses, expand, w_o, x, mod)


def _attention_layer(x, mod, g, w_qkv, q_gain, k_gain, w_o, cos_t, sin_t, cos_p, sin_p):
    he = ATTN_HEADS * HEAD_DIM
    gains = jnp.stack([jnp.tile(q_gain, ATTN_HEADS), jnp.tile(k_gain, ATTN_HEADS),
                       jnp.ones((he,), F32)]).reshape(3, 1, he)
    qkv0 = _qkv_proj(x, mod, g, w_qkv[:, :3 * he], gains, cos_t, sin_t, 1)
    qkv12 = _qkv_proj(x, mod, g, w_qkv[:, 3 * he:], gains, cos_p, sin_p, PLANES)
    outs, lses = zip(_dilated_attention(qkv0, 0), _dilated_attention(qkv12, 1), _dilated_attention(qkv12, 2))
    return _attn_merge_proj(outs, lses, w_o, x, mod)


def _pw1_glu_kernel(x_ref, mod_ref, g_ref, wa_ref, wb_ref, ba_ref, bb_ref, o_ref, h_scr):
    j = pl.program_id(2)

    @pl.when(j == 0)
    def _():
        h = _normmod(x_ref[...], g_ref[...], mod_ref[1:2, :], mod_ref[0:1, :])
        h_scr[...] = h.astype(BF16)

    h = h_scr[...]
    a = _dot(h, wa_ref[...]) + ba_ref[...]
    b = _dot(h, wb_ref[...]) + bb_ref[...]
    o_ref[...] = a * _sigmoid(b)


def _pw1_glu(x, mod, g, w_pw1, b_pw1):
    bsz, s, d = x.shape
    ch = w_pw1.shape[1] // 2
    tm, tn = 1024, 512
    nj = ch // tn
    b2 = b_pw1.reshape(1, 2 * ch)
    return pl.pallas_call(
        _pw1_glu_kernel,
        grid=(bsz, s // tm, nj),
        in_specs=[
            pl.BlockSpec((None, tm, d), lambda b, i, j: (b, i, 0)),
            pl.BlockSpec((None, 6, d), lambda b, i, j: (b, 0, 0)),
            pl.BlockSpec((1, d), lambda b, i, j: (0, 0)),
            pl.BlockSpec((d, tn), lambda b, i, j: (0, j)),
            pl.BlockSpec((d, tn), lambda b, i, j: (0, j + nj)),
            pl.BlockSpec((1, tn), lambda b, i, j: (0, j)),
            pl.BlockSpec((1, tn), lambda b, i, j: (0, j + nj)),
        ],
        out_specs=pl.BlockSpec((None, tm, tn), lambda b, i, j: (b, i, j)),
        out_shape=jax.ShapeDtypeStruct((bsz, s, ch), F32),
        scratch_shapes=[pltpu.VMEM((tm, d), BF16)],
        compiler_params=_cparams(("parallel", "parallel", "arbitrary")),
        name="conv_pw1_glu",
    )(x, mod, g.reshape(1, d), w_pw1, w_pw1, b2, b2)


CONV_HALO = 32
CONV_ROWS = 16
SUBLANES = 8


def _dwconv_kernel(u_ref, halo_ref, wdw_ref, bdw_ref, lng_ref, lnb_ref, w2_ref, b2_ref, x_ref, mod_ref,
                   o_ref, ubuf, cbuf):
    i = pl.program_id(1)
    tm = u_ref.shape[0]
    span = tm + CONV_HALO - SUBLANES
    halo = halo_ref[...]
    ubuf[0, 0:CONV_HALO, :] = jnp.where(i > 0, halo, jnp.zeros_like(halo))
    ubuf[0, CONV_HALO:, :] = u_ref[...]
    for sft in range(1, SUBLANES):
        ubuf[sft, 0:span, :] = ubuf[0, sft:sft + span, :]
    first = CONV_HALO - (CONV_WIDTH - 1)
    for c in range(tm // CONV_ROWS):
        acc = jnp.zeros((CONV_ROWS, u_ref.shape[1]), F32)
        for k in range(CONV_WIDTH):
            off = c * CONV_ROWS + first + k
            sft = off % SUBLANES
            acc = acc + wdw_ref[k:k + 1, :] * ubuf[sft, off - sft:off - sft + CONV_ROWS, :]
        cbuf[c * CONV_ROWS:(c + 1) * CONV_ROWS, :] = acc
    u = cbuf[...] + bdw_ref[...]
    mu = jnp.mean(u, axis=-1, keepdims=True)
    uc = u - mu
    var = jnp.mean(uc * uc, axis=-1, keepdims=True)
    v = uc * lax.rsqrt(var + NORM_EPS) * lng_ref[...] + lnb_ref[...]
    z = (v * _sigmoid(v)).astype(BF16)
    y = _dot(z, w2_ref[...]) + b2_ref[...]
    o_ref[...] = x_ref[...] + mod_ref[2:3, :] * y


def _dwconv_ln_pw2(u, w_dw, b_dw, ln_g, ln_b, w_pw2, b_pw2, x, mod):
    bsz, s, d = x.shape
    ch = u.shape[-1]
    tm = 256
    ratio = tm // CONV_HALO
    wpad = jnp.zeros((32, ch), F32).at[:CONV_WIDTH].set(w_dw)
    row = lambda a: a.reshape(1, -1)
    return pl.pallas_call(
        _dwconv_kernel,
        grid=(bsz, s // tm),
        in_specs=[
            pl.BlockSpec((None, tm, ch), lambda b, i: (b, i, 0)),
            pl.BlockSpec((None, CONV_HALO, ch), lambda b, i: (b, jnp.maximum(i * ratio - 1, 0), 0)),
            pl.BlockSpec((32, ch), lambda b, i: (0, 0)),
            pl.BlockSpec((1, ch), lambda b, i: (0, 0)),
            pl.BlockSpec((1, ch), lambda b, i: (0, 0)),
            pl.BlockSpec((1, ch), lambda b, i: (0, 0)),
            pl.BlockSpec((ch, d), lambda b, i: (0, 0)),
            pl.BlockSpec((1, d), lambda b, i: (0, 0)),
            pl.BlockSpec((None, tm, d), lambda b, i: (b, i, 0)),
            pl.BlockSpec((None, 6, d), lambda b, i: (b, 0, 0)),
        ],
        out_specs=pl.BlockSpec((None, tm, d), lambda b, i: (b, i, 0)),
        out_shape=jax.ShapeDtypeStruct(x.shape, F32),
        scratch_shapes=[pltpu.VMEM((SUBLANES, tm + CONV_HALO, ch), F32), pltpu.VMEM((tm, ch), F32)],
        compiler_params=_cparams(("parallel", "parallel")),
        name="conv_dw_ln_pw2",
    )(u, u, wpad, row(b_dw), row(ln_g), row(ln_b), w_pw2, row(b_pw2), x, mod)


def _conv_layer(x, mod, g, w_pw1, b_pw1, w_dw, b_dw, ln_g, ln_b, w_pw2, b_pw2):
    u = _pw1_glu(x, mod, g, w_pw1, b_pw1)
    return _dwconv_ln_pw2(u, w_dw, b_dw, ln_g, ln_b, w_pw2, b_pw2, x, mod)


def _rwkv_proj_kernel(x_ref, halo_ref, mod_ref, g_ref, mu_ref, wr_ref, wk_ref, wv_ref,
                      wa_ref, wb_ref, aa_ref, ab_ref, ga_ref, gb_ref, vec_ref,
                      r_ref, ld_ref, k_ref, v_ref, kk_ref, a_ref, gate_ref):
    i = pl.program_id(1)
    g = g_ref[...]
    scale, shift = mod_ref[1:2, :], mod_ref[0:1, :]
    h = _normmod(x_ref[...], g, scale, shift)
    h_halo = _normmod(halo_ref[...], g, scale, shift)
    last = jnp.where(i > 0, h_halo[7:8, :], jnp.zeros_like(h_halo[7:8, :]))
    rows = lax.broadcasted_iota(jnp.int32, h.shape, 0)
    h_prev = jnp.where(rows == 0, last, pltpu.roll(h, 1, 0))
    xx = h_prev - h

    def mix(m):
        return (h + xx * mu_ref[m:m + 1, :]).astype(BF16)

    w0, a0, k_k, k_a = vec_ref[0:1, :], vec_ref[1:2, :], vec_ref[2:3, :], vec_ref[3:4, :]
    r = _dot(mix(0), wr_ref[...])
    k = _dot(mix(2), wk_ref[...])
    v = _dot(mix(3), wv_ref[...])
    zw = w0 + _dot(jnp.tanh(_dot(mix(1), wa_ref[...])).astype(BF16), wb_ref[...])
    w = -(jnp.maximum(-zw, 0.0) + jnp.log1p(jnp.exp(-jnp.abs(zw)))) - 0.5
    a = _sigmoid(a0 + _dot(_dot(mix(4), aa_ref[...]).astype(BF16), ab_ref[...]))
    gate = _dot(_sigmoid(_dot(mix(5), ga_ref[...])).astype(BF16), gb_ref[...])
    kk = k * k_k
    ones = _head_ones(LANE_GROUP)
    sq = (kk * kk).astype(BF16)
    d = kk.shape[1]
    ss = jnp.concatenate([_dot(sq[:, c:c + LANE_GROUP], ones) for c in range(0, d, LANE_GROUP)], axis=1)
    kk = kk / jnp.maximum(jnp.sqrt(ss), 1e-12)
    r_ref[...] = r
    ld_ref[...] = -jnp.exp(w)
    k_ref[...] = k * (1.0 + (a - 1.0) * k_a)
    v_ref[...] = v
    kk_ref[...] = kk
    a_ref[...] = a
    gate_ref[...] = gate


def _rwkv_proj(x, mod, g, mu, w_r, w_k, w_v, wa, wb, aa, ab, ga, gb, vecs):
    bsz, s, d = x.shape
    tm = 256
    ratio = tm // 8
    full = lambda a: pl.BlockSpec(a.shape, lambda b, i: (0,) * a.ndim)
    tok = pl.BlockSpec((None, tm, d), lambda b, i: (b, i, 0))
    mu8 = jnp.zeros((8, d), F32).at[:6].set(mu)
    weights = [w_r, w_k, w_v, wa, wb, aa, ab, ga, gb, vecs]
    return pl.pallas_call(
        _rwkv_proj_kernel,
        grid=(bsz, s // tm),
        in_specs=[tok,
                  pl.BlockSpec((None, 8, d), lambda b, i: (b, jnp.maximum(i * ratio - 1, 0), 0)),
                  pl.BlockSpec((None, 6, d), lambda b, i: (b, 0, 0)),
                  pl.BlockSpec((1, d), lambda b, i: (0, 0)),
                  full(mu8)] + [full(a) for a in weights],
        out_specs=[tok] * 7,
        out_shape=[jax.ShapeDtypeStruct(x.shape, F32)] * 7,
        compiler_params=_cparams(("parallel", "parallel")),
        name="rwkv_proj",
    )(x, x, mod, g.reshape(1, d), mu8, *weights)


def _wkv_kernel(r_ref, ld_ref, k_ref, v_ref, kk_ref, a_ref, gate_ref, vec_ref, o_ref, state):
    t = pl.program_id(1)

    @pl.when(t == 0)
    def _():
        state[...] = jnp.zeros_like(state)

    c = CHUNK
    w = LANE_GROUP
    nh = w // HEAD_DIM
    n_streams = r_ref.shape[0]
    n_chunks = r_ref.shape[1] // c
    n_groups = r_ref.shape[2] // w
    lane_sl = [slice(g * w, (g + 1) * w) for g in range(n_groups)]
    assert c == HEAD_DIM

    lane_head = _head_of((c, w), 1)
    head_masks = [lane_head == h for h in range(nh)]
    row_c = lax.broadcasted_iota(jnp.int32, (c, w), 0)
    col_in = _pos_in_head((c, w), 1)
    strict = col_in < row_c
    incl = col_in <= row_c
    eye = (col_in == row_c).astype(F32)
    tri = (lax.broadcasted_iota(jnp.int32, (c, c), 1) <= lax.broadcasted_iota(jnp.int32, (c, c), 0)).astype(BF16)
    diag_blocks = _head_of((w, w), 0) == _head_of((w, w), 1)
    ones = diag_blocks.astype(BF16)
    zero16 = jnp.zeros((), BF16)

    def bd(m):
        m16 = m.astype(BF16)
        return jnp.concatenate([jnp.where(msk, m16, zero16) for msk in head_masks], axis=0)

    def bd_cols(m):
        m16 = m.astype(BF16)
        return jnp.where(diag_blocks, jnp.concatenate([m16] * nh, axis=0), zero16)

    def mm(a, b):
        return _dot(a.astype(BF16), b.astype(BF16))

    def prefix_sum(x):
        h1 = x.astype(BF16)
        r1 = x - h1.astype(F32)
        h2 = r1.astype(BF16)
        h3 = (r1 - h2.astype(F32)).astype(BF16)
        return _dot(tri, h1) + _dot(tri, h2) + _dot(tri, h3)

    def each(fn, *lists):
        return [fn(*args) for args in zip(*lists)]

    def body(ci, carry):
        rows = pl.ds(pl.multiple_of(ci * c, c), c)
        units = [(s, lane_sl[g]) for s in range(n_streams) for g in range(n_groups)]
        ld = [ld_ref[s, rows, ln] for s, ln in units]
        kk = [kk_ref[s, rows, ln] for s, ln in units]
        v = [v_ref[s, rows, ln] for s, ln in units]

        h1 = each(lambda x: x.astype(BF16), ld)
        r1 = each(lambda x, h: x - h.astype(F32), ld, h1)
        h2 = each(lambda x: x.astype(BF16), r1)
        h3 = each(lambda x, h: (x - h.astype(F32)).astype(BF16), r1, h2)
        c1 = each(lambda h: _dot(tri, h), h1)
        c2 = each(lambda h: _dot(tri, h), h2)
        c3 = each(lambda h: _dot(tri, h), h3)
        cum = each(lambda x, y, z: x + y + z, c1, c2, c3)
        p_in = each(jnp.exp, cum)
        p_inv = each(lambda x: jnp.exp(-x), cum)
        p_end = each(lambda p: p[c - 1:c, :], p_in)
        at = each(lambda q, x, l: -q * jnp.exp(x - l), kk, cum, ld)
        bt = [q * a_ref[s, rows, ln] * pi for (s, ln), q, pi in zip(units, kk, p_inv)]
        kt = [k_ref[s, rows, ln] * pi for (s, ln), pi in zip(units, p_inv)]
        lhs = [jnp.concatenate([x, r_ref[s, rows, ln] * p], axis=0).astype(BF16)
               for (s, ln), x, p in zip(units, at, p_in)]
        bt_bd = each(bd, bt)
        kt_bd = each(bd, kt)
        g_b = each(_dot_t, lhs, bt_bd)
        g_k = each(_dot_t, lhs, kt_bd)
        a_ab = each(lambda g: jnp.where(strict, g[:c], 0.0), g_b)
        a_ak = each(lambda g: jnp.where(strict, g[:c], 0.0).astype(BF16), g_k)
        b_cat = each(lambda gb, gk: jnp.concatenate(
            [jnp.where(incl, gb[c:], 0.0), jnp.where(incl, gk[c:], 0.0)], axis=1).astype(BF16), g_b, g_k)

        x_inv = each(lambda m: eye + m, a_ab)
        pw = each(lambda m: mm(m, bd_cols(m)), a_ab)
        for _ in range(4):
            both = each(lambda x, p: mm(jnp.concatenate([x, p], axis=0), bd_cols(p)), x_inv, pw)
            x_inv = each(lambda x, b: x + b[:c], x_inv, both)
            pw = each(lambda b: b[c:], both)
        x_inv = each(lambda x, p: (x + mm(x, bd_cols(p))).astype(BF16), x_inv, pw)

        v_bd = each(bd, v)
        akv = each(_dot, a_ak, v_bd)
        bk = each(lambda b, k, p: jnp.concatenate([b * p, k * p], axis=0).astype(BF16), bt, kt, p_end)
        rkr = [(r_ref[s, rows, ln] * k_ref[s, rows, ln] * vec_ref[0:1, ln]).astype(BF16) for s, ln in units]
        bonus = each(lambda x, val: _dot(x, ones) * val, rkr, v)

        s_bd = [state[s, g] for s in range(n_streams) for g in range(n_groups)]
        sa_sr = each(lambda l, st: _dot_t(l, st.astype(BF16)), lhs, s_bd)
        u_rhs = each(lambda x, y: bd(x[:c] + y), sa_sr, akv)
        u = each(_dot, x_inv, u_rhs)
        uv_t = each(lambda x, val: jnp.concatenate([x, val], axis=0).T.astype(BF16), u, v)
        upd = each(_dot, uv_t, bk)
        s_new = each(lambda st, p, x: st * p + jnp.where(diag_blocks, x, 0.0), s_bd, p_end, upd)
        i = 0
        for s in range(n_streams):
            for g in range(n_groups):
                state[s, g] = s_new[i]
                i += 1
        y_rhs = each(lambda x, vb: jnp.concatenate([bd(x), vb], axis=0), u, v_bd)
        y2 = each(_dot, b_cat, y_rhs)
        y = each(lambda x, z: x[c:] + z, sa_sr, y2)

        inv_n = 1.0 / HEAD_DIM
        y_hi = each(lambda x: x.astype(BF16), y)
        y_lo = each(lambda x, h: (x - h.astype(F32)).astype(BF16), y, y_hi)
        m_hi = each(lambda h: _dot(h, ones), y_hi)
        m_lo = each(lambda h: _dot(h, ones), y_lo)
        yc = each(lambda x, a, b: x - (a + b) * inv_n, y, m_hi, m_lo)
        var = each(lambda x: _dot((x * x).astype(BF16), ones) * inv_n, yc)
        for (s, ln), x, vr, bo in zip(units, yc, var, bonus):
            yn = x * lax.rsqrt(vr + RWKV_GN_EPS) * vec_ref[1:2, ln] + vec_ref[2:3, ln]
            o_ref[s, rows, ln] = ((yn + bo) * gate_ref[s, rows, ln]).astype(o_ref.dtype)
        return carry

    lax.fori_loop(0, n_chunks, body, 0)


WKV_TOKENS = 256


def _wkv_scan(r, ld, k, v, kk, a, gate, vecs):
    bsz, s, d = r.shape
    tb, lw = WKV_TOKENS, d
    tok = pl.BlockSpec((bsz, tb, lw), lambda gi, t: (0, t, gi))
    return pl.pallas_call(
        _wkv_kernel,
        grid=(d // lw, s // tb),
        in_specs=[tok] * 7 + [pl.BlockSpec((8, lw), lambda gi, t: (0, gi))],
        out_specs=tok,
        out_shape=jax.ShapeDtypeStruct((bsz, s, d), BF16),
        scratch_shapes=[pltpu.VMEM((bsz, lw // LANE_GROUP, LANE_GROUP, LANE_GROUP), F32)],
        compiler_params=_cparams(("parallel", "arbitrary")),
        name="wkv7_scan",
    )(r, ld, k, v, kk, a, gate, vecs)


def _rwkv_layer(x, mod, g, mu, w_r, w_k, w_v, w_o, w0, wa, wb, a0, aa, ab, ga, gb, k_k, k_a, r_k, ln_g, ln_b):
    d = x.shape[-1]
    vecs = jnp.zeros((8, d), F32).at[0].set(w0).at[1].set(a0).at[2].set(k_k).at[3].set(k_a)
    r, ld, k, v, kk, a, gate = _rwkv_proj(x, mod, g, mu, w_r, w_k, w_v, wa, wb, aa, ab, ga, gb, vecs)
    svecs = jnp.zeros((8, d), F32).at[0].set(r_k.reshape(d)).at[1].set(ln_g).at[2].set(ln_b)
    z = _wkv_scan(r, ld, k, v, kk, a, gate, svecs)
    return _proj_residual(z, w_o, jnp.zeros((d,), F32), x, mod)


def _rope_tables(positions):
    half = HEAD_DIM // 2
    inv_freq = jnp.power(ROPE_THETA, -jnp.arange(0, HEAD_DIM, 2, dtype=F32) / HEAD_DIM)
    ang = positions.astype(F32)[..., None] * inv_freq
    cos, sin = jnp.cos(ang), jnp.sin(ang)
    assert cos.shape[-1] == half
    return jnp.concatenate([cos, cos, cos, cos], -1), jnp.concatenate([-sin, sin, -sin, sin], -1)


def _plane_major(t):
    bsz, s, n = t.shape
    return t.reshape(bsz, s // PLANES, PLANES, n).transpose(0, 2, 1, 3)


def kernel(x, c, positions, ada_w, ada_b, norm_mix_g, norm_ffn_g, ffn_w_gate, ffn_w_up, ffn_w_down, attn_w_qkv, attn_q_gain, attn_k_gain, attn_w_o, conv_w_pw1, conv_b_pw1, conv_w_dw, conv_b_dw, conv_ln_g, conv_ln_b, conv_w_pw2, conv_b_pw2, rwkv_mu, rwkv_w_r, rwkv_w_k, rwkv_w_v, rwkv_w_o, rwkv_w0, rwkv_w_lora_a, rwkv_w_lora_b, rwkv_a0, rwkv_a_lora_a, rwkv_a_lora_b, rwkv_g_lora_a, rwkv_g_lora_b, rwkv_k_k, rwkv_k_a, rwkv_r_k, rwkv_ln_g, rwkv_ln_b):
    depth = ada_w.shape[0]
    bf = lambda a: a.astype(BF16)
    cos_t, sin_t = _rope_tables(positions)
    cos_p, sin_p = _plane_major(cos_t), _plane_major(sin_t)
    mods = _ada_mod(c, ada_w, ada_b)
    for i in range(depth):
        mod = mods[i]
        kind, j = i % 3, i // 3
        if kind == 0:
            x = _attention_layer(x, mod, norm_mix_g[i], bf(attn_w_qkv[j]), attn_q_gain[j], attn_k_gain[j],
                                 bf(attn_w_o[j]), cos_t, sin_t, cos_p, sin_p)
        elif kind == 1:
            x = _conv_layer(x, mod, norm_mix_g[i], bf(conv_w_pw1[j]), conv_b_pw1[j], conv_w_dw[j], conv_b_dw[j],
                            conv_ln_g[j], conv_ln_b[j], bf(conv_w_pw2[j]), conv_b_pw2[j])
        else:
            x = _rwkv_layer(x, mod, norm_mix_g[i], rwkv_mu[j], bf(rwkv_w_r[j]), bf(rwkv_w_k[j]), bf(rwkv_w_v[j]),
                            bf(rwkv_w_o[j]), rwkv_w0[j], bf(rwkv_w_lora_a[j]), bf(rwkv_w_lora_b[j]), rwkv_a0[j],
                            bf(rwkv_a_lora_a[j]), bf(rwkv_a_lora_b[j]), bf(rwkv_g_lora_a[j]), bf(rwkv_g_lora_b[j]),
                            rwkv_k_k[j], rwkv_k_a[j], rwkv_r_k[j], rwkv_ln_g[j], rwkv_ln_b[j])
        x = _ffn(x, mod, norm_ffn_g[i], bf(ffn_w_gate[i]), bf(ffn_w_up[i]), bf(ffn_w_down[i]))
    return x
```

```python
import functools

import jax
import jax.numpy as jnp
from jax import lax
from jax.experimental import pallas as pl
from jax.experimental.pallas import tpu as pltpu

F32 = jnp.float32
BF16 = jnp.bfloat16

NORM_EPS = 1e-6
HEAD_DIM = 64
ATTN_HEADS = 16
DILATED_GROUPS = ((128, 1), (512, 4), (2048, 16))
ATTN_BLOCK = 128
ROPE_THETA = 10000.0
NEG_INF = -1e30
CONV_WIDTH = 31
RWKV_GN_EPS = 1e-5 * HEAD_DIM
CHUNK = 64
LANE_GROUP = 256
VMEM_LIMIT = 56 * 1024 * 1024


def _cparams(sem):
    return pltpu.CompilerParams(dimension_semantics=sem, vmem_limit_bytes=VMEM_LIMIT)


def _dot(a, b):
    return jnp.dot(a, b, preferred_element_type=F32)


def _dot_t(a, b):
    return lax.dot_general(a, b, (((1,), (1,)), ((), ())), preferred_element_type=F32)


def _split_dot(x, w):
    hi = x.astype(BF16)
    lo = (x - hi.astype(F32)).astype(BF16)
    return _dot(hi, w) + _dot(lo, w)


def _sigmoid(x):
    return 1.0 / (1.0 + jnp.exp(-x))


def _normmod(x, g, scale, shift):
    ms = jnp.mean(x * x, axis=-1, keepdims=True)
    return x * lax.rsqrt(ms + NORM_EPS) * g * (1.0 + scale) + shift


HEAD_SHIFT = HEAD_DIM.bit_length() - 1
assert 1 << HEAD_SHIFT == HEAD_DIM


def _head_of(shape, axis):
    return jnp.right_shift(lax.broadcasted_iota(jnp.int32, shape, axis), HEAD_SHIFT)


def _pos_in_head(shape, axis):
    return jnp.bitwise_and(lax.broadcasted_iota(jnp.int32, shape, axis), HEAD_DIM - 1)


def _head_ones(n):
    return (_head_of((n, n), 0) == _head_of((n, n), 1)).astype(BF16)


def _ada_kernel(c_ref, w_ref, b_ref, o_ref):
    c = c_ref[...]
    c_act = (c * _sigmoid(c)).astype(BF16)
    o_ref[...] = _dot(c_act, w_ref[...].astype(BF16)) + b_ref[...]


def _ada_mod(c, ada_w, ada_b):
    depth, d, n = ada_w.shape
    bsz = c.shape[0]
    rows = 8
    c_pad = jnp.zeros((rows, d), F32).at[:bsz].set(c)
    tn = 1536
    out = pl.pallas_call(
        _ada_kernel,
        grid=(depth, n // tn),
        in_specs=[
            pl.BlockSpec((rows, d), lambda i, j: (0, 0)),
            pl.BlockSpec((None, d, tn), lambda i, j: (i, 0, j)),
            pl.BlockSpec((None, 1, tn), lambda i, j: (i, 0, j)),
        ],
        out_specs=pl.BlockSpec((None, rows, tn), lambda i, j: (i, 0, j)),
        out_shape=jax.ShapeDtypeStruct((depth, rows, n), F32),
        compiler_params=_cparams(("parallel", "parallel")),
        name="ada_mod",
    )(c_pad, ada_w, ada_b.reshape(depth, 1, n))
    return out[:, :bsz].reshape(depth, bsz, 6, d)


FFN_SUB = 256


def _ffn_kernel(x_ref, mod_ref, g_ref, wg_ref, wu_ref, wd_ref, o_ref, h_scr, acc_scr):
    j = pl.program_id(2)

    @pl.when(j == 0)
    def _():
        h = _normmod(x_ref[...], g_ref[...], mod_ref[4:5, :], mod_ref[3:4, :])
        h_scr[...] = h.astype(BF16)
        acc_scr[...] = jnp.zeros_like(acc_scr)

    h = h_scr[...]
    tf = wg_ref.shape[1]
    cols = [(c0, min(c0 + FFN_SUB, tf)) for c0 in range(0, tf, FFN_SUB)]
    gate_up = lambda c: (_dot(h, wg_ref[:, c[0]:c[1]]), _dot(h, wu_ref[:, c[0]:c[1]]))
    nxt = gate_up(cols[0])
    for idx, c in enumerate(cols):
        a, u = nxt
        if idx + 1 < len(cols):
            nxt = gate_up(cols[idx + 1])
        z = (a * _sigmoid(a) * u).astype(BF16)
        acc_scr[...] += _dot(z, wd_ref[c[0]:c[1], :])

    @pl.when(j == pl.num_programs(2) - 1)
    def _():
        o_ref[...] = x_ref[...] + mod_ref[5:6, :] * acc_scr[...]


def _ffn(x, mod, g, wg, wu, wd):
    bsz, s, d = x.shape
    f = wg.shape[1]
    tm, tf = 1024, f // 2
    return pl.pallas_call(
        _ffn_kernel,
        grid=(bsz, s // tm, f // tf),
        in_specs=[
            pl.BlockSpec((None, tm, d), lambda b, i, j: (b, i, 0)),
            pl.BlockSpec((None, 6, d), lambda b, i, j: (b, 0, 0)),
            pl.BlockSpec((1, d), lambda b, i, j: (0, 0)),
            pl.BlockSpec((d, tf), lambda b, i, j: (0, j)),
            pl.BlockSpec((d, tf), lambda b, i, j: (0, j)),
            pl.BlockSpec((tf, d), lambda b, i, j: (j, 0)),
        ],
        out_specs=pl.BlockSpec((None, tm, d), lambda b, i, j: (b, i, 0)),
        out_shape=jax.ShapeDtypeStruct(x.shape, F32),
        scratch_shapes=[pltpu.VMEM((tm, d), BF16), pltpu.VMEM((tm, d), F32)],
        compiler_params=_cparams(("parallel", "parallel", "arbitrary")),
        name="ffn",
    )(x, mod, g.reshape(1, d), wg, wu, wd)


def _proj_res_kernel(z_ref, w_ref, b_ref, x_ref, mod_ref, o_ref):
    y = _dot(z_ref[...], w_ref[...]) + b_ref[...]
    o_ref[...] = x_ref[...] + mod_ref[2:3, :] * y


def _proj_residual(z, w, bias, x, mod):
    bsz, s, d = x.shape
    k = z.shape[-1]
    tm = 1024
    return pl.pallas_call(
        _proj_res_kernel,
        grid=(bsz, s // tm),
        in_specs=[
            pl.BlockSpec((None, tm, k), lambda b, i: (b, i, 0)),
            pl.BlockSpec((k, d), lambda b, i: (0, 0)),
            pl.BlockSpec((1, d), lambda b, i: (0, 0)),
            pl.BlockSpec((None, tm, d), lambda b, i: (b, i, 0)),
            pl.BlockSpec((None, 6, d), lambda b, i: (b, 0, 0)),
        ],
        out_specs=pl.BlockSpec((None, tm, d), lambda b, i: (b, i, 0)),
        out_shape=jax.ShapeDtypeStruct(x.shape, F32),
        compiler_params=_cparams(("parallel", "parallel")),
        name="proj_residual",
    )(z, w, bias.reshape(1, d), x, mod)


PLANES = DILATED_GROUPS[-1][1]
ATTN_TM = 1024


LANES = 128


def _to_planes(val, scr, planes):
    tm, n = val.shape
    rows = tm // planes
    for c in range(n // LANES):
        scr[c] = val[:, c * LANES:(c + 1) * LANES]
    return [jnp.concatenate([scr[c, pl.ds(p, rows, stride=planes), :] for c in range(n // LANES)], axis=1)
            for p in range(planes)]


def _from_planes(val, scr, planes):
    tm, n = val.shape
    rows = tm // planes
    for p in range(planes):
        for c in range(n // LANES):
            scr[c, pl.ds(p, rows, stride=planes), :] = val[p * rows:(p + 1) * rows, c * LANES:(c + 1) * LANES]
    return jnp.concatenate([scr[c] for c in range(n // LANES)], axis=1)


QK_ROWS = 32


def _block_rows_ref(ref, r0, nrows):
    if len(ref.shape) == 2:
        return ref.at[r0:r0 + nrows, :]
    per = ref.shape[1]
    return ref.at[r0 // per, r0 % per:r0 % per + nrows, :]


def _rms_rope_store(y_ref, ss_ref, gain_ref, cos_ref, sin_ref, o_ref, plain):
    tm, n = ss_ref.shape
    tw = cos_ref.shape[-1]
    half = HEAD_DIM // 2
    gain = gain_ref[:, 0:tw]
    gain_rot = pltpu.roll(gain, half, 1)
    lane_lo = _pos_in_head((QK_ROWS, n), 1) < half
    for r0 in range(0, tm, QK_ROWS):
        y = jnp.concatenate([y_ref[c, r0:r0 + QK_ROWS, :] for c in range(n // LANES)], axis=1)
        r = jnp.where(plain, 1.0, lax.rsqrt(ss_ref[r0:r0 + QK_ROWS, :] * (1.0 / HEAD_DIM) + NORM_EPS))
        cos = jnp.where(plain, 1.0, _block_rows_ref(cos_ref, r0, QK_ROWS)[...] * gain)
        sin = jnp.where(plain, 0.0, _block_rows_ref(sin_ref, r0, QK_ROWS)[...] * gain_rot)
        cos = jnp.concatenate([cos] * (n // tw), axis=1)
        sin = jnp.concatenate([sin] * (n // tw), axis=1)
        partner = jnp.where(lane_lo, pltpu.roll(y, n - half, 1), pltpu.roll(y, half, 1))
        _block_rows_ref(o_ref, r0, QK_ROWS)[...] = (r * (y * cos + partner * sin)).astype(o_ref.dtype)


def _qkv_kernel(x_ref, mod_ref, g_ref, w_ref, gain_ref, cos_ref, sin_ref, o_ref, h_scr, y_scr, ss_scr,
                *, planes, n_chunks, n_steps):
    t = pl.program_id(0)
    tm = x_ref.shape[0]
    j = jnp.minimum(t, n_steps - 2) % n_chunks
    slot = t % 2

    @pl.when(t == 0)
    def _():
        y_scr[...] = jnp.zeros_like(y_scr)
        ss_scr[...] = jnp.zeros_like(ss_scr)

    @pl.when(jnp.logical_and(j == 0, t < n_steps - 1))
    def _():
        h = _normmod(x_ref[...], g_ref[...], mod_ref[1:2, :], mod_ref[0:1, :])
        if planes == 1:
            h_scr[...] = h.astype(BF16)
        else:
            rows = tm // planes
            for p, hp in enumerate(_to_planes(h, y_scr.at[slot], planes)):
                h_scr[p * rows:(p + 1) * rows, :] = hp.astype(BF16)

    prev_kind = (jnp.maximum(t - 1, 0) % n_chunks) % 3
    ones = _head_ones(LANE_GROUP)

    def step(cur, prev):
        _rms_rope_store(y_scr.at[prev], ss_scr.at[prev], gain_ref, cos_ref, sin_ref, o_ref, prev_kind == 2)
        y = _dot(h_scr[...], w_ref[...])
        n = y.shape[1]
        for c in range(n // LANES):
            y_scr[cur, c] = y[:, c * LANES:(c + 1) * LANES]
        sq = (y * y).astype(BF16)
        for c in range(0, n, LANE_GROUP):
            ss_scr[cur, :, c:c + LANE_GROUP] = _dot(sq[:, c:c + LANE_GROUP], ones)

    pl.when(slot == 0)(lambda: step(0, 1))
    pl.when(slot == 1)(lambda: step(1, 0))


def _qkv_proj(x, mod, g, w, gains, cos_t, sin_t, planes):
    bsz, s, d = x.shape
    n = w.shape[1]
    tn = ATTN_HEADS * HEAD_DIM
    tm = ATTN_TM
    tw = cos_t.shape[-1]
    assert tn == d
    n_tiles, n_chunks = s // tm, n // tn
    total = bsz * n_tiles * n_chunks

    def where(t):
        tile, j = t // n_chunks, t % n_chunks
        return tile // n_tiles, tile % n_tiles, j

    cur = lambda t: where(jnp.minimum(t, total - 1))
    prev = lambda t: where(jnp.maximum(t - 1, 0))
    if planes == 1:
        tok = lambda width: pl.BlockSpec((None, tm, width), lambda t: (prev(t)[0], prev(t)[1], 0))
        out_spec = pl.BlockSpec((None, tm, tn), lambda t: prev(t))
        out_shape = jax.ShapeDtypeStruct((bsz, s, n), BF16)
    else:
        rows = tm // planes
        tok = lambda width: pl.BlockSpec((None, planes, rows, width), lambda t: (prev(t)[0], 0, prev(t)[1], 0))
        out_spec = pl.BlockSpec((None, planes, rows, tn), lambda t: (prev(t)[0], 0, prev(t)[1], prev(t)[2]))
        out_shape = jax.ShapeDtypeStruct((bsz, planes, s // planes, n), BF16)
    return pl.pallas_call(
        functools.partial(_qkv_kernel, planes=planes, n_chunks=n_chunks, n_steps=total + 1),
        grid=(total + 1,),
        in_specs=[
            pl.BlockSpec((None, tm, d), lambda t: (cur(t)[0], cur(t)[1], 0)),
            pl.BlockSpec((None, 6, d), lambda t: (cur(t)[0], 0, 0)),
            pl.BlockSpec((1, d), lambda t: (0, 0)),
            pl.BlockSpec((d, tn), lambda t: (0, cur(t)[2])),
            pl.BlockSpec((None, 1, tn), lambda t: (prev(t)[2] % 3, 0, 0)),
            tok(tw), tok(tw),
        ],
        out_specs=out_spec,
        out_shape=out_shape,
        scratch_shapes=[pltpu.VMEM((tm, d), BF16), pltpu.VMEM((2, tn // LANES, tm, LANES), F32),
                        pltpu.VMEM((2, tm, tn), F32)],
        compiler_params=_cparams(("arbitrary",)),
        name=f"attn_qkv_p{planes}",
    )(x, mod, g.reshape(1, d), w, gains, cos_t, sin_t)


ATTN_STEP_BLOCKS = 2


def _attn_kernel(q_ref, k_ref, v_ref, kp_ref, vp_ref, o_ref, lse_ref, *, interleave):
    n = pl.program_id(2)
    blk = ATTN_BLOCK
    he = ATTN_HEADS * HEAD_DIM
    seg = blk // interleave

    def pos(axis):
        i = lax.broadcasted_iota(jnp.int32, (blk, blk), axis)
        if interleave == 1:
            return i
        return interleave * jnp.bitwise_and(i, seg - 1) + jnp.right_shift(i, seg.bit_length() - 1)

    def block_of(ref, b):
        if len(ref.shape) == 2:
            return ref[b * blk:(b + 1) * blk, :]
        return ref[:, b * seg:(b + 1) * seg, :].reshape(blk, he)

    qpos, kpos = pos(0), pos(1)
    cur_ok = kpos <= qpos
    prev_ok = kpos >= qpos
    first_ok = kpos >= qpos + jnp.where(n > 0, 0, blk)
    pair = 2 * HEAD_DIM
    lane = lax.broadcasted_iota(jnp.int32, (blk, pair), 1)
    first = lane < HEAD_DIM
    zero = jnp.zeros((), BF16)
    scale = jnp.asarray(HEAD_DIM ** -0.5, BF16)
    sl = lambda hp: slice(hp * pair, (hp + 1) * pair)
    n_blocks = ATTN_STEP_BLOCKS

    q = [block_of(q_ref, b) * scale for b in range(n_blocks)]
    kc = [block_of(k_ref, b) for b in range(n_blocks)]
    vc = [block_of(v_ref, b) for b in range(n_blocks)]
    kp = [kp_ref[...].reshape(blk, he)] + kc[:-1]
    vp = [vp_ref[...].reshape(blk, he)] + vc[:-1]
    masks = [first_ok] + [prev_ok] * (n_blocks - 1)

    units = [(b, hp, sub) for b in range(n_blocks) for hp in range(ATTN_HEADS // 2) for sub in range(2)]
    qh = [jnp.where(first if sub == 0 else jnp.logical_not(first), q[b][:, sl(hp)], zero) for b, hp, sub in units]
    s_c = [jnp.where(cur_ok, _dot_t(x, kc[b][:, sl(hp)]), NEG_INF) for x, (b, hp, _) in zip(qh, units)]
    s_p = [jnp.where(masks[b], _dot_t(x, kp[b][:, sl(hp)]), NEG_INF) for x, (b, hp, _) in zip(qh, units)]
    m = [jnp.max(jnp.maximum(a, c), axis=-1, keepdims=True) for a, c in zip(s_c, s_p)]
    p_c = [jnp.exp(a - mx) for a, mx in zip(s_c, m)]
    p_p = [jnp.exp(a - mx) for a, mx in zip(s_p, m)]
    l = [jnp.sum(a + c, axis=-1, keepdims=True) for a, c in zip(p_c, p_p)]
    pv = [_dot(a.astype(BF16), vc[b][:, sl(hp)]) + _dot(c.astype(BF16), vp[b][:, sl(hp)])
          for a, c, (b, hp, _) in zip(p_c, p_p, units)]
    o = [x / dd for x, dd in zip(pv, l)]
    lse_lane = lax.broadcasted_iota(jnp.int32, (blk, lse_ref.shape[-1]), 1)
    per_block = ATTN_HEADS
    for b in range(n_blocks):
        ob = o[b * per_block:(b + 1) * per_block]
        out = jnp.concatenate([jnp.where(first, ob[2 * hp], ob[2 * hp + 1]) for hp in range(ATTN_HEADS // 2)], axis=1)
        lse_tile = jnp.zeros((blk, lse_ref.shape[-1]), F32)
        for idx in range(per_block):
            u = b * per_block + idx
            lse_tile = jnp.where(lse_lane == idx, m[u] + jnp.log(l[u]), lse_tile)
        if len(o_ref.shape) == 2:
            o_ref[b * blk:(b + 1) * blk, :] = out.astype(o_ref.dtype)
            lse_ref[b * blk:(b + 1) * blk, :] = lse_tile
        else:
            o_ref[:, b * seg:(b + 1) * seg, :] = out.astype(o_ref.dtype).reshape(interleave, seg, he)
            lse_ref[:, b * seg:(b + 1) * seg, :] = lse_tile.reshape(interleave, seg, lse_ref.shape[-1])


LSE_LANES = 128


def _dilated_attention(qkv, group):
    he = ATTN_HEADS * HEAD_DIM
    blk = ATTN_BLOCK
    nb = ATTN_STEP_BLOCKS
    dilation = DILATED_GROUPS[group][1]
    if group == 0:
        bsz, s, _ = qkv.shape
        grid = (bsz, 1, s // (nb * blk))
        block = lambda rows, width: (None, rows, width)
        imap = lambda col: (lambda b, r, i: (b, i, col))
        imap_prev = lambda col: (lambda b, r, i: (b, jnp.maximum(nb * i - 1, 0), col))
        out_dims = (bsz, s)
        base, interleave, view, seg = 0, 1, qkv, blk
    else:
        bsz, planes, l2, n = qkv.shape
        interleave = planes // dilation
        seg = blk // interleave
        view = qkv.reshape(bsz, interleave, dilation, l2, n)
        grid = (bsz, dilation, l2 // (nb * seg))
        block = lambda rows, width: (None, interleave, None, rows, width)
        imap = lambda col: (lambda b, r, i: (b, 0, r, i, col))
        imap_prev = lambda col: (lambda b, r, i: (b, 0, r, jnp.maximum(nb * i - 1, 0), col))
        out_dims = (bsz, interleave, dilation, l2)
        base = 3 * (group - 1)
    cur = lambda which: pl.BlockSpec(block(nb * seg, he), imap(base + which))
    prev = lambda which: pl.BlockSpec(block(seg, he), imap_prev(base + which))
    o, lse = pl.pallas_call(
        functools.partial(_attn_kernel, interleave=interleave),
        grid=grid,
        in_specs=[cur(0), cur(1), cur(2), prev(1), prev(2)],
        out_specs=[pl.BlockSpec(block(nb * seg, he), imap(0)), pl.BlockSpec(block(nb * seg, LSE_LANES), imap(0))],
        out_shape=[jax.ShapeDtypeStruct(out_dims + (he,), BF16), jax.ShapeDtypeStruct(out_dims + (LSE_LANES,), F32)],
        compiler_params=_cparams(("parallel", "parallel", "parallel")),
        name=f"dilated_attn_g{group}",
    )(view, view, view, view, view)
    if group == 0:
        return o, lse
    return o.reshape(bsz, planes, l2, he), lse.reshape(bsz, planes, l2, LSE_LANES)


def _merge_kernel(o0_ref, o1_ref, o2_ref, l0_ref, l1_ref, l2_ref, e_ref, w_ref, x_ref, mod_ref, out_ref,
                  tok_scr, lse_scr):
    planes, rows, he = o1_ref.shape
    tm = planes * rows
    l0 = jnp.concatenate(_to_planes(l0_ref[...], lse_scr, planes), axis=0)
    l1 = l1_ref[...].reshape(tm, LSE_LANES)
    l2 = l2_ref[...].reshape(tm, LSE_LANES)
    m = jnp.maximum(jnp.maximum(l0, l1), l2)
    e0, e1, e2 = jnp.exp(l0 - m), jnp.exp(l1 - m), jnp.exp(l2 - m)
    inv = 1.0 / (e0 + e1 + e2)
    expand = e_ref[...]
    part = (_split_dot(e1 * inv, expand) * o1_ref[...].reshape(tm, he).astype(F32)
            + _split_dot(e2 * inv, expand) * o2_ref[...].reshape(tm, he).astype(F32))
    w0 = _from_planes(e0 * inv, lse_scr, planes)
    o = _split_dot(w0, expand) * o0_ref[...].astype(F32) + _from_planes(part, tok_scr, planes)
    y = _dot(o.astype(BF16), w_ref[...])
    out_ref[...] = x_ref[...] + mod_ref[2:3, :] * y


def _attn_merge_proj(outs, lses, w_o, x, mod):
    bsz, s, d = x.shape
    he = outs[0].shape[-1]
    tm = ATTN_TM
    rows = tm // PLANES
    head_of_lane = jnp.arange(he, dtype=jnp.int32) // HEAD_DIM
    expand = (jnp.arange(LSE_LANES, dtype=jnp.int32)[:, None] == head_of_lane[None, :]).astype(BF16)
    tok = lambda width: pl.BlockSpec((None, tm, width), lambda b, i: (b, i, 0))
    pm = lambda width: pl.BlockSpec((None, PLANES, rows, width), lambda b, i: (b, 0, i, 0))
    return pl.pallas_call(
        _merge_kernel,
        grid=(bsz, s // tm),
        in_specs=[tok(he), pm(he), pm(he), tok(LSE_LANES), pm(LSE_LANES), pm(LSE_LANES),
                  pl.BlockSpec((LSE_LANES, he), lambda b, i: (0, 0)),
                  pl.BlockSpec((he, d), lambda b, i: (0, 0)),
                  tok(d),
                  pl.BlockSpec((None, 6, d), lambda b, i: (b, 0, 0))],
        out_specs=tok(d),
        out_shape=jax.ShapeDtypeStruct(x.shape, F32),
        scratch_shapes=[pltpu.VMEM((he // LANES, tm, LANES), F32), pltpu.VMEM((LSE_LANES // LANES, tm, LANES), F32)],
        compiler_params=_cparams(("parallel", "parallel")),
        name="attn_merge_proj",
    )(*outs, *lses, expand, w_o, x, mod)


def _attention_layer(x, mod, g, w_qkv, q_gain, k_gain, w_o, cos_t, sin_t, cos_p, sin_p):
    he = ATTN_HEADS * HEAD_DIM
    gains = jnp.stack([jnp.tile(q_gain, ATTN_HEADS), jnp.tile(k_gain, ATTN_HEADS),
                       jnp.ones((he,), F32)]).reshape(3, 1, he)
    qkv0 = _qkv_proj(x, mod, g, w_qkv[:, :3 * he], gains, cos_t, sin_t, 1)
    qkv12 = _qkv_proj(x, mod, g, w_qkv[:, 3 * he:], gains, cos_p, sin_p, PLANES)
    outs, lses = zip(_dilated_attention(qkv0, 0), _dilated_attention(qkv12, 1), _dilated_attention(qkv12, 2))
    return _attn_merge_proj(outs, lses, w_o, x, mod)


def _pw1_glu_kernel(x_ref, mod_ref, g_ref, wa_ref, wb_ref, ba_ref, bb_ref, o_ref, h_scr):
    j = pl.program_id(2)

    @pl.when(j == 0)
    def _():
        h = _normmod(x_ref[...], g_ref[...], mod_ref[1:2, :], mod_ref[0:1, :])
        h_scr[...] = h.astype(BF16)

    h = h_scr[...]
    a = _dot(h, wa_ref[...]) + ba_ref[...]
    b = _dot(h, wb_ref[...]) + bb_ref[...]
    o_ref[...] = a * _sigmoid(b)


def _pw1_glu(x, mod, g, w_pw1, b_pw1):
    bsz, s, d = x.shape
    ch = w_pw1.shape[1] // 2
    tm, tn = 1024, 512
    nj = ch // tn
    b2 = b_pw1.reshape(1, 2 * ch)
    return pl.pallas_call(
        _pw1_glu_kernel,
        grid=(bsz, s // tm, nj),
        in_specs=[
            pl.BlockSpec((None, tm, d), lambda b, i, j: (b, i, 0)),
            pl.BlockSpec((None, 6, d), lambda b, i, j: (b, 0, 0)),
            pl.BlockSpec((1, d), lambda b, i, j: (0, 0)),
            pl.BlockSpec((d, tn), lambda b, i, j: (0, j)),
            pl.BlockSpec((d, tn), lambda b, i, j: (0, j + nj)),
            pl.BlockSpec((1, tn), lambda b, i, j: (0, j)),
            pl.BlockSpec((1, tn), lambda b, i, j: (0, j + nj)),
        ],
        out_specs=pl.BlockSpec((None, tm, tn), lambda b, i, j: (b, i, j)),
        out_shape=jax.ShapeDtypeStruct((bsz, s, ch), F32),
        scratch_shapes=[pltpu.VMEM((tm, d), BF16)],
        compiler_params=_cparams(("parallel", "parallel", "arbitrary")),
        name="conv_pw1_glu",
    )(x, mod, g.reshape(1, d), w_pw1, w_pw1, b2, b2)


CONV_HALO = 32
CONV_ROWS = 16
SUBLANES = 8


def _dwconv_kernel(u_ref, halo_ref, wdw_ref, bdw_ref, lng_ref, lnb_ref, w2_ref, b2_ref, x_ref, mod_ref,
                   o_ref, ubuf, cbuf):
    i = pl.program_id(1)
    tm = u_ref.shape[0]
    span = tm + CONV_HALO - SUBLANES
    halo = halo_ref[...]
    ubuf[0, 0:CONV_HALO, :] = jnp.where(i > 0, halo, jnp.zeros_like(halo))
    ubuf[0, CONV_HALO:, :] = u_ref[...]
    for sft in range(1, SUBLANES):
        ubuf[sft, 0:span, :] = ubuf[0, sft:sft + span, :]
    first = CONV_HALO - (CONV_WIDTH - 1)
    for c in range(tm // CONV_ROWS):
        accs = [jnp.zeros((SUBLANES, u_ref.shape[1]), F32) for _ in range(CONV_ROWS // SUBLANES)]
        for k in range(CONV_WIDTH):
            off = c * CONV_ROWS + first + k
            sft = off % SUBLANES
            w8 = wdw_ref[k * SUBLANES:(k + 1) * SUBLANES, :]
            for a in range(len(accs)):
                lo = off - sft + a * SUBLANES
                accs[a] = accs[a] + w8 * ubuf[sft, lo:lo + SUBLANES, :]
        for a, acc in enumerate(accs):
            cbuf[c * CONV_ROWS + a * SUBLANES:c * CONV_ROWS + (a + 1) * SUBLANES, :] = acc
    u = cbuf[...] + bdw_ref[...]
    mu = jnp.mean(u, axis=-1, keepdims=True)
    uc = u - mu
    var = jnp.mean(uc * uc, axis=-1, keepdims=True)
    v = uc * lax.rsqrt(var + NORM_EPS) * lng_ref[...] + lnb_ref[...]
    z = (v * _sigmoid(v)).astype(BF16)
    y = _dot(z, w2_ref[...]) + b2_ref[...]
    o_ref[...] = x_ref[...] + mod_ref[2:3, :] * y


def _dwconv_ln_pw2(u, w_dw, b_dw, ln_g, ln_b, w_pw2, b_pw2, x, mod):
    bsz, s, d = x.shape
    ch = u.shape[-1]
    tm = 256
    ratio = tm // CONV_HALO
    wrep = jnp.repeat(w_dw, SUBLANES, axis=0)
    row = lambda a: a.reshape(1, -1)
    return pl.pallas_call(
        _dwconv_kernel,
        grid=(bsz, s // tm),
        in_specs=[
            pl.BlockSpec((None, tm, ch), lambda b, i: (b, i, 0)),
            pl.BlockSpec((None, CONV_HALO, ch), lambda b, i: (b, jnp.maximum(i * ratio - 1, 0), 0)),
            pl.BlockSpec(wrep.shape, lambda b, i: (0, 0)),
            pl.BlockSpec((1, ch), lambda b, i: (0, 0)),
            pl.BlockSpec((1, ch), lambda b, i: (0, 0)),
            pl.BlockSpec((1, ch), lambda b, i: (0, 0)),
            pl.BlockSpec((ch, d), lambda b, i: (0, 0)),
            pl.BlockSpec((1, d), lambda b, i: (0, 0)),
            pl.BlockSpec((None, tm, d), lambda b, i: (b, i, 0)),
            pl.BlockSpec((None, 6, d), lambda b, i: (b, 0, 0)),
        ],
        out_specs=pl.BlockSpec((None, tm, d), lambda b, i: (b, i, 0)),
        out_shape=jax.ShapeDtypeStruct(x.shape, F32),
        scratch_shapes=[pltpu.VMEM((SUBLANES, tm + CONV_HALO, ch), F32), pltpu.VMEM((tm, ch), F32)],
        compiler_params=_cparams(("parallel", "parallel")),
        name="conv_dw_ln_pw2",
    )(u, u, wrep, row(b_dw), row(ln_g), row(ln_b), w_pw2, row(b_pw2), x, mod)


def _conv_layer(x, mod, g, w_pw1, b_pw1, w_dw, b_dw, ln_g, ln_b, w_pw2, b_pw2):
    u = _pw1_glu(x, mod, g, w_pw1, b_pw1)
    return _dwconv_ln_pw2(u, w_dw, b_dw, ln_g, ln_b, w_pw2, b_pw2, x, mod)


def _rwkv_proj_kernel(x_ref, halo_ref, mod_ref, g_ref, mu_ref, wr_ref, wk_ref, wv_ref,
                      wa_ref, wb_ref, aa_ref, ab_ref, ga_ref, gb_ref, vec_ref,
                      r_ref, ld_ref, k_ref, v_ref, kk_ref, a_ref, gate_ref):
    i = pl.program_id(1)
    g = g_ref[...]
    scale, shift = mod_ref[1:2, :], mod_ref[0:1, :]
    h = _normmod(x_ref[...], g, scale, shift)
    h_halo = _normmod(halo_ref[...], g, scale, shift)
    last = jnp.where(i > 0, h_halo[7:8, :], jnp.zeros_like(h_halo[7:8, :]))
    rows = lax.broadcasted_iota(jnp.int32, h.shape, 0)
    h_prev = jnp.where(rows == 0, last, pltpu.roll(h, 1, 0))
    xx = h_prev - h

    def mix(m):
        return (h + xx * mu_ref[m:m + 1, :]).astype(BF16)

    w0, a0, k_k, k_a = vec_ref[0:1, :], vec_ref[1:2, :], vec_ref[2:3, :], vec_ref[3:4, :]
    r = _dot(mix(0), wr_ref[...])
    k = _dot(mix(2), wk_ref[...])
    v = _dot(mix(3), wv_ref[...])
    zw = w0 + _dot(jnp.tanh(_dot(mix(1), wa_ref[...])).astype(BF16), wb_ref[...])
    w = -(jnp.maximum(-zw, 0.0) + jnp.log1p(jnp.exp(-jnp.abs(zw)))) - 0.5
    a = _sigmoid(a0 + _dot(_dot(mix(4), aa_ref[...]).astype(BF16), ab_ref[...]))
    gate = _dot(_sigmoid(_dot(mix(5), ga_ref[...])).astype(BF16), gb_ref[...])
    kk = k * k_k
    ones = _head_ones(LANE_GROUP)
    sq = (kk * kk).astype(BF16)
    d = kk.shape[1]
    ss = jnp.concatenate([_dot(sq[:, c:c + LANE_GROUP], ones) for c in range(0, d, LANE_GROUP)], axis=1)
    kk = kk / jnp.maximum(jnp.sqrt(ss), 1e-12)
    r_ref[...] = r.astype(r_ref.dtype)
    ld_ref[...] = -jnp.exp(w)
    k_ref[...] = (k * (1.0 + (a - 1.0) * k_a)).astype(k_ref.dtype)
    v_ref[...] = v.astype(v_ref.dtype)
    kk_ref[...] = kk.astype(kk_ref.dtype)
    a_ref[...] = a.astype(a_ref.dtype)
    gate_ref[...] = gate.astype(gate_ref.dtype)


def _rwkv_proj(x, mod, g, mu, w_r, w_k, w_v, wa, wb, aa, ab, ga, gb, vecs):
    bsz, s, d = x.shape
    tm = 256
    ratio = tm // 8
    full = lambda a: pl.BlockSpec(a.shape, lambda b, i: (0,) * a.ndim)
    tok = pl.BlockSpec((None, tm, d), lambda b, i: (b, i, 0))
    mu8 = jnp.zeros((8, d), F32).at[:6].set(mu)
    weights = [w_r, w_k, w_v, wa, wb, aa, ab, ga, gb, vecs]
    return pl.pallas_call(
        _rwkv_proj_kernel,
        grid=(bsz, s // tm),
        in_specs=[tok,
                  pl.BlockSpec((None, 8, d), lambda b, i: (b, jnp.maximum(i * ratio - 1, 0), 0)),
                  pl.BlockSpec((None, 6, d), lambda b, i: (b, 0, 0)),
                  pl.BlockSpec((1, d), lambda b, i: (0, 0)),
                  full(mu8)] + [full(a) for a in weights],
        out_specs=[tok] * 7,
        out_shape=[jax.ShapeDtypeStruct(x.shape, F32 if i == 1 else BF16) for i in range(7)],
        compiler_params=_cparams(("parallel", "parallel")),
        name="rwkv_proj",
    )(x, x, mod, g.reshape(1, d), mu8, *weights)


def _wkv_kernel(r_ref, ld_ref, k_ref, v_ref, kk_ref, a_ref, gate_ref, vec_ref, o_ref, state):
    t = pl.program_id(1)

    @pl.when(t == 0)
    def _():
        state[...] = jnp.zeros_like(state)

    c = CHUNK
    w = LANE_GROUP
    nh = w // HEAD_DIM
    n_streams = r_ref.shape[0]
    n_chunks = r_ref.shape[1] // c
    n_groups = r_ref.shape[2] // w
    lane_sl = [slice(g * w, (g + 1) * w) for g in range(n_groups)]
    assert c == HEAD_DIM

    lane_head = _head_of((c, w), 1)
    head_masks = [lane_head == h for h in range(nh)]
    row_c = lax.broadcasted_iota(jnp.int32, (c, w), 0)
    col_in = _pos_in_head((c, w), 1)
    strict = col_in < row_c
    incl = col_in <= row_c
    eye = (col_in == row_c).astype(F32)
    tri = (lax.broadcasted_iota(jnp.int32, (c, c), 1) <= lax.broadcasted_iota(jnp.int32, (c, c), 0)).astype(BF16)
    diag_blocks = _head_of((w, w), 0) == _head_of((w, w), 1)
    ones = diag_blocks.astype(BF16)
    zero16 = jnp.zeros((), BF16)

    def bd(m):
        m16 = m.astype(BF16)
        return jnp.concatenate([jnp.where(msk, m16, zero16) for msk in head_masks], axis=0)

    def bd_cols(m):
        m16 = m.astype(BF16)
        return jnp.where(diag_blocks, jnp.concatenate([m16] * nh, axis=0), zero16)

    def mm(a, b):
        return _dot(a.astype(BF16), b.astype(BF16))

    def prefix_sum(x):
        h1 = x.astype(BF16)
        r1 = x - h1.astype(F32)
        h2 = r1.astype(BF16)
        h3 = (r1 - h2.astype(F32)).astype(BF16)
        return _dot(tri, h1) + _dot(tri, h2) + _dot(tri, h3)

    def each(fn, *lists):
        return [fn(*args) for args in zip(*lists)]

    def body(ci, carry):
        rows = pl.ds(pl.multiple_of(ci * c, c), c)
        units = [(s, lane_sl[g]) for s in range(n_streams) for g in range(n_groups)]
        ld = [ld_ref[s, rows, ln] for s, ln in units]
        kk = [kk_ref[s, rows, ln].astype(F32) for s, ln in units]
        v = [v_ref[s, rows, ln].astype(F32) for s, ln in units]

        h1 = each(lambda x: x.astype(BF16), ld)
        r1 = each(lambda x, h: x - h.astype(F32), ld, h1)
        h2 = each(lambda x: x.astype(BF16), r1)
        h3 = each(lambda x, h: (x - h.astype(F32)).astype(BF16), r1, h2)
        c1 = each(lambda h: _dot(tri, h), h1)
        c2 = each(lambda h: _dot(tri, h), h2)
        c3 = each(lambda h: _dot(tri, h), h3)
        cum = each(lambda x, y, z: x + y + z, c1, c2, c3)
        p_in = each(jnp.exp, cum)
        p_inv = each(lambda x: jnp.exp(-x), cum)
        p_end = each(lambda p: p[c - 1:c, :], p_in)
        at = each(lambda q, x, l: -q * jnp.exp(x - l), kk, cum, ld)
        bt = [q * a_ref[s, rows, ln].astype(F32) * pi for (s, ln), q, pi in zip(units, kk, p_inv)]
        kt = [k_ref[s, rows, ln].astype(F32) * pi for (s, ln), pi in zip(units, p_inv)]
        lhs = [jnp.concatenate([x, r_ref[s, rows, ln].astype(F32) * p], axis=0).astype(BF16)
               for (s, ln), x, p in zip(units, at, p_in)]
        bt_bd = each(bd, bt)
        kt_bd = each(bd, kt)
        g_b = each(_dot_t, lhs, bt_bd)
        g_k = each(_dot_t, lhs, kt_bd)
        a_ab = each(lambda g: jnp.where(strict, g[:c], 0.0), g_b)
        a_ak = each(lambda g: jnp.where(strict, g[:c], 0.0).astype(BF16), g_k)
        b_cat = each(lambda gb, gk: jnp.concatenate(
            [jnp.where(incl, gb[c:], 0.0), jnp.where(incl, gk[c:], 0.0)], axis=1).astype(BF16), g_b, g_k)

        x_inv = each(lambda m: eye + m, a_ab)
        pw = each(lambda m: mm(m, bd_cols(m)), a_ab)
        for _ in range(4):
            both = each(lambda x, p: mm(jnp.concatenate([x, p], axis=0), bd_cols(p)), x_inv, pw)
            x_inv = each(lambda x, b: x + b[:c], x_inv, both)
            pw = each(lambda b: b[c:], both)
        x_inv = each(lambda x, p: (x + mm(x, bd_cols(p))).astype(BF16), x_inv, pw)

        v_bd = each(bd, v)
        akv = each(_dot, a_ak, v_bd)
        bk = each(lambda b, k, p: jnp.concatenate([b * p, k * p], axis=0).astype(BF16), bt, kt, p_end)
        rkr = [(r_ref[s, rows, ln].astype(F32) * k_ref[s, rows, ln].astype(F32) * vec_ref[0:1, ln]).astype(BF16)
               for s, ln in units]
        bonus = each(lambda x, val: _dot(x, ones) * val, rkr, v)

        s_bd = [state[s, g] for s in range(n_streams) for g in range(n_groups)]
        sa_sr = each(lambda l, st: _dot_t(l, st.astype(BF16)), lhs, s_bd)
        u_rhs = each(lambda x, y: bd(x[:c] + y), sa_sr, akv)
        u = each(_dot, x_inv, u_rhs)
        uv_t = each(lambda x, val: jnp.concatenate([x, val], axis=0).T.astype(BF16), u, v)
        upd = each(_dot, uv_t, bk)
        s_new = each(lambda st, p, x: st * p + jnp.where(diag_blocks, x, 0.0), s_bd, p_end, upd)
        i = 0
        for s in range(n_streams):
            for g in range(n_groups):
                state[s, g] = s_new[i]
                i += 1
        y_rhs = each(lambda x, vb: jnp.concatenate([bd(x), vb], axis=0), u, v_bd)
        y2 = each(_dot, b_cat, y_rhs)
        y = each(lambda x, z: x[c:] + z, sa_sr, y2)

        inv_n = 1.0 / HEAD_DIM
        y_hi = each(lambda x: x.astype(BF16), y)
        y_lo = each(lambda x, h: (x - h.astype(F32)).astype(BF16), y, y_hi)
        m_hi = each(lambda h: _dot(h, ones), y_hi)
        m_lo = each(lambda h: _dot(h, ones), y_lo)
        yc = each(lambda x, a, b: x - (a + b) * inv_n, y, m_hi, m_lo)
        var = each(lambda x: _dot((x * x).astype(BF16), ones) * inv_n, yc)
        for (s, ln), x, vr, bo in zip(units, yc, var, bonus):
            yn = x * lax.rsqrt(vr + RWKV_GN_EPS) * vec_ref[1:2, ln] + vec_ref[2:3, ln]
            o_ref[s, rows, ln] = ((yn + bo) * gate_ref[s, rows, ln].astype(F32)).astype(o_ref.dtype)
        return carry

    lax.fori_loop(0, n_chunks, body, 0)


WKV_TOKENS = 256


def _wkv_scan(r, ld, k, v, kk, a, gate, vecs):
    bsz, s, d = r.shape
    tb, lw = WKV_TOKENS, d
    tok = pl.BlockSpec((bsz, tb, lw), lambda gi, t: (0, t, gi))
    return pl.pallas_call(
        _wkv_kernel,
        grid=(d // lw, s // tb),
        in_specs=[tok] * 7 + [pl.BlockSpec((8, lw), lambda gi, t: (0, gi))],
        out_specs=tok,
        out_shape=jax.ShapeDtypeStruct((bsz, s, d), BF16),
        scratch_shapes=[pltpu.VMEM((bsz, lw // LANE_GROUP, LANE_GROUP, LANE_GROUP), F32)],
        compiler_params=_cparams(("parallel", "arbitrary")),
        name="wkv7_scan",
    )(r, ld, k, v, kk, a, gate, vecs)


def _rwkv_layer(x, mod, g, mu, w_r, w_k, w_v, w_o, w0, wa, wb, a0, aa, ab, ga, gb, k_k, k_a, r_k, ln_g, ln_b):
    d = x.shape[-1]
    vecs = jnp.zeros((8, d), F32).at[0].set(w0).at[1].set(a0).at[2].set(k_k).at[3].set(k_a)
    r, ld, k, v, kk, a, gate = _rwkv_proj(x, mod, g, mu, w_r, w_k, w_v, wa, wb, aa, ab, ga, gb, vecs)
    svecs = jnp.zeros((8, d), F32).at[0].set(r_k.reshape(d)).at[1].set(ln_g).at[2].set(ln_b)
    z = _wkv_scan(r, ld, k, v, kk, a, gate, svecs)
    return _proj_residual(z, w_o, jnp.zeros((d,), F32), x, mod)


def _rope_tables(positions):
    half = HEAD_DIM // 2
    inv_freq = jnp.power(ROPE_THETA, -jnp.arange(0, HEAD_DIM, 2, dtype=F32) / HEAD_DIM)
    ang = positions.astype(F32)[..., None] * inv_freq
    cos, sin = jnp.cos(ang), jnp.sin(ang)
    assert cos.shape[-1] == half
    return jnp.concatenate([cos, cos, cos, cos], -1), jnp.concatenate([-sin, sin, -sin, sin], -1)


def _plane_major(t):
    bsz, s, n = t.shape
    return t.reshape(bsz, s // PLANES, PLANES, n).transpose(0, 2, 1, 3)


def kernel(x, c, positions, ada_w, ada_b, norm_mix_g, norm_ffn_g, ffn_w_gate, ffn_w_up, ffn_w_down, attn_w_qkv, attn_q_gain, attn_k_gain, attn_w_o, conv_w_pw1, conv_b_pw1, conv_w_dw, conv_b_dw, conv_ln_g, conv_ln_b, conv_w_pw2, conv_b_pw2, rwkv_mu, rwkv_w_r, rwkv_w_k, rwkv_w_v, rwkv_w_o, rwkv_w0, rwkv_w_lora_a, rwkv_w_lora_b, rwkv_a0, rwkv_a_lora_a, rwkv_a_lora_b, rwkv_g_lora_a, rwkv_g_lora_b, rwkv_k_k, rwkv_k_a, rwkv_r_k, rwkv_ln_g, rwkv_ln_b):
    depth = ada_w.shape[0]
    bf = lambda a: a.astype(BF16)
    cos_t, sin_t = _rope_tables(positions)
    cos_p, sin_p = _plane_major(cos_t), _plane_major(sin_t)
    mods = _ada_mod(c, ada_w, ada_b)
    for i in range(depth):
        mod = mods[i]
        kind, j = i % 3, i // 3
        if kind == 0:
            x = _attention_layer(x, mod, norm_mix_g[i], bf(attn_w_qkv[j]), attn_q_gain[j], attn_k_gain[j],
                                 bf(attn_w_o[j]), cos_t, sin_t, cos_p, sin_p)
        elif kind == 1:
            x = _conv_layer(x, mod, norm_mix_g[i], bf(conv_w_pw1[j]), conv_b_pw1[j], conv_w_dw[j], conv_b_dw[j],
                            conv_ln_g[j], conv_ln_b[j], bf(conv_w_pw2[j]), conv_b_pw2[j])
        else:
            x = _rwkv_layer(x, mod, norm_mix_g[i], rwkv_mu[j], bf(rwkv_w_r[j]), bf(rwkv_w_k[j]), bf(rwkv_w_v[j]),
                            bf(rwkv_w_o[j]), rwkv_w0[j], bf(rwkv_w_lora_a[j]), bf(rwkv_w_lora_b[j]), rwkv_a0[j],
                            bf(rwkv_a_lora_a[j]), bf(rwkv_a_lora_b[j]), bf(rwkv_g_lora_a[j]), bf(rwkv_g_lora_b[j]),
                            rwkv_k_k[j], rwkv_k_a[j], rwkv_r_k[j], rwkv_ln_g[j], rwkv_ln_b[j])
        x = _ffn(x, mod, norm_ffn_g[i], bf(ffn_w_gate[i]), bf(ffn_w_up[i]), bf(ffn_w_down[i]))
    return x
```

```python
import functools

import jax
import jax.numpy as jnp
from jax import lax
from jax.experimental import pallas as pl
from jax.experimental.pallas import tpu as pltpu

F32 = jnp.float32
BF16 = jnp.bfloat16

NORM_EPS = 1e-6
HEAD_DIM = 64
ATTN_HEADS = 16
DILATED_GROUPS = ((128, 1), (512, 4), (2048, 16))
ATTN_BLOCK = 128
ROPE_THETA = 10000.0
NEG_INF = -1e30
CONV_WIDTH = 31
RWKV_GN_EPS = 1e-5 * HEAD_DIM
CHUNK = 64
LANE_GROUP = 256
VMEM_LIMIT = 56 * 1024 * 1024


def _cparams(sem):
    return pltpu.CompilerParams(dimension_semantics=sem, vmem_limit_bytes=VMEM_LIMIT)


def _dot(a, b):
    return jnp.dot(a, b, preferred_element_type=F32)


def _dot_t(a, b):
    return lax.dot_general(a, b, (((1,), (1,)), ((), ())), preferred_element_type=F32)


def _sigmoid(x):
    return 1.0 / (1.0 + jnp.exp(-x))


def _normmod(x, g, scale, shift):
    ms = jnp.mean(x * x, axis=-1, keepdims=True)
    return x * lax.rsqrt(ms + NORM_EPS) * g * (1.0 + scale) + shift


HEAD_SHIFT = HEAD_DIM.bit_length() - 1
assert 1 << HEAD_SHIFT == HEAD_DIM


def _head_of(shape, axis):
    return jnp.right_shift(lax.broadcasted_iota(jnp.int32, shape, axis), HEAD_SHIFT)


def _pos_in_head(shape, axis):
    return jnp.bitwise_and(lax.broadcasted_iota(jnp.int32, shape, axis), HEAD_DIM - 1)


def _head_ones(n):
    return (_head_of((n, n), 0) == _head_of((n, n), 1)).astype(BF16)


def _ada_kernel(c_ref, w_ref, b_ref, o_ref):
    c = c_ref[...]
    c_act = (c * _sigmoid(c)).astype(BF16)
    o_ref[...] = _dot(c_act, w_ref[...].astype(BF16)) + b_ref[...]


def _ada_mod(c, ada_w, ada_b):
    depth, d, n = ada_w.shape
    bsz = c.shape[0]
    rows = 8
    c_pad = jnp.zeros((rows, d), F32).at[:bsz].set(c)
    tn = 1536
    out = pl.pallas_call(
        _ada_kernel,
        grid=(depth, n // tn),
        in_specs=[
            pl.BlockSpec((rows, d), lambda i, j: (0, 0)),
            pl.BlockSpec((None, d, tn), lambda i, j: (i, 0, j)),
            pl.BlockSpec((None, 1, tn), lambda i, j: (i, 0, j)),
        ],
        out_specs=pl.BlockSpec((None, rows, tn), lambda i, j: (i, 0, j)),
        out_shape=jax.ShapeDtypeStruct((depth, rows, n), F32),
        compiler_params=_cparams(("parallel", "parallel")),
        name="ada_mod",
    )(c_pad, ada_w, ada_b.reshape(depth, 1, n))
    return out[:, :bsz].reshape(depth, bsz, 6, d)


FFN_SUB = 256


def _ffn_kernel(x_ref, mod_ref, g_ref, wg_ref, wu_ref, wd_ref, o_ref, h_scr, acc_scr):
    j = pl.program_id(2)

    @pl.when(j == 0)
    def _():
        h = _normmod(x_ref[...], g_ref[...], mod_ref[4:5, :], mod_ref[3:4, :])
        h_scr[...] = h.astype(BF16)
        acc_scr[...] = jnp.zeros_like(acc_scr)

    h = h_scr[...]
    tf = wg_ref.shape[1]
    cols = [(c0, min(c0 + FFN_SUB, tf)) for c0 in range(0, tf, FFN_SUB)]
    gate_up = lambda c: (_dot(h, wg_ref[:, c[0]:c[1]]), _dot(h, wu_ref[:, c[0]:c[1]]))
    nxt = gate_up(cols[0])
    for idx, c in enumerate(cols):
        a, u = nxt
        if idx + 1 < len(cols):
            nxt = gate_up(cols[idx + 1])
        z = (a * _sigmoid(a) * u).astype(BF16)
        acc_scr[...] += _dot(z, wd_ref[c[0]:c[1], :])

    @pl.when(j == pl.num_programs(2) - 1)
    def _():
        o_ref[...] = x_ref[...] + mod_ref[5:6, :] * acc_scr[...]


def _ffn(x, mod, g, wg, wu, wd):
    bsz, s, d = x.shape
    f = wg.shape[1]
    tm, tf = 1024, f // 2
    return pl.pallas_call(
        _ffn_kernel,
        grid=(bsz, s // tm, f // tf),
        in_specs=[
            pl.BlockSpec((None, tm, d), lambda b, i, j: (b, i, 0)),
            pl.BlockSpec((None, 6, d), lambda b, i, j: (b, 0, 0)),
            pl.BlockSpec((1, d), lambda b, i, j: (0, 0)),
            pl.BlockSpec((d, tf), lambda b, i, j: (0, j)),
            pl.BlockSpec((d, tf), lambda b, i, j: (0, j)),
            pl.BlockSpec((tf, d), lambda b, i, j: (j, 0)),
        ],
        out_specs=pl.BlockSpec((None, tm, d), lambda b, i, j: (b, i, 0)),
        out_shape=jax.ShapeDtypeStruct(x.shape, F32),
        scratch_shapes=[pltpu.VMEM((tm, d), BF16), pltpu.VMEM((tm, d), F32)],
        compiler_params=_cparams(("parallel", "parallel", "arbitrary")),
        name="ffn",
    )(x, mod, g.reshape(1, d), wg, wu, wd)


def _proj_res_kernel(z_ref, w_ref, b_ref, x_ref, mod_ref, o_ref):
    y = _dot(z_ref[...], w_ref[...]) + b_ref[...]
    o_ref[...] = x_ref[...] + mod_ref[2:3, :] * y


def _proj_residual(z, w, bias, x, mod):
    bsz, s, d = x.shape
    k = z.shape[-1]
    tm = 1024
    return pl.pallas_call(
        _proj_res_kernel,
        grid=(bsz, s // tm),
        in_specs=[
            pl.BlockSpec((None, tm, k), lambda b, i: (b, i, 0)),
            pl.BlockSpec((k, d), lambda b, i: (0, 0)),
            pl.BlockSpec((1, d), lambda b, i: (0, 0)),
            pl.BlockSpec((None, tm, d), lambda b, i: (b, i, 0)),
            pl.BlockSpec((None, 6, d), lambda b, i: (b, 0, 0)),
        ],
        out_specs=pl.BlockSpec((None, tm, d), lambda b, i: (b, i, 0)),
        out_shape=jax.ShapeDtypeStruct(x.shape, F32),
        compiler_params=_cparams(("parallel", "parallel")),
        name="proj_residual",
    )(z, w, bias.reshape(1, d), x, mod)


PLANES = DILATED_GROUPS[-1][1]
ATTN_TM = 1024


LANES = 128


def _to_planes(val, scr, planes):
    tm, n = val.shape
    rows = tm // planes
    for c in range(n // LANES):
        scr[c] = val[:, c * LANES:(c + 1) * LANES]
    return [jnp.concatenate([scr[c, pl.ds(p, rows, stride=planes), :] for c in range(n // LANES)], axis=1)
            for p in range(planes)]


def _from_planes(val, scr, planes):
    tm, n = val.shape
    rows = tm // planes
    for p in range(planes):
        for c in range(n // LANES):
            scr[c, pl.ds(p, rows, stride=planes), :] = val[p * rows:(p + 1) * rows, c * LANES:(c + 1) * LANES]
    return jnp.concatenate([scr[c] for c in range(n // LANES)], axis=1)


QK_ROWS = 32


def _block_rows_ref(ref, r0, nrows):
    if len(ref.shape) == 2:
        return ref.at[r0:r0 + nrows, :]
    per = ref.shape[1]
    return ref.at[r0 // per, r0 % per:r0 % per + nrows, :]


def _rms_rope_store(y_ref, ss_ref, gain_ref, cos_ref, sin_ref, o_ref, plain):
    tm, n = ss_ref.shape
    tw = cos_ref.shape[-1]
    half = HEAD_DIM // 2
    gain = gain_ref[:, 0:tw]
    gain_rot = pltpu.roll(gain, half, 1)
    lane_lo = _pos_in_head((QK_ROWS, n), 1) < half
    for r0 in range(0, tm, QK_ROWS):
        y = jnp.concatenate([y_ref[c, r0:r0 + QK_ROWS, :] for c in range(n // LANES)], axis=1)
        r = jnp.where(plain, 1.0, lax.rsqrt(ss_ref[r0:r0 + QK_ROWS, :] * (1.0 / HEAD_DIM) + NORM_EPS))
        cos = jnp.where(plain, 1.0, _block_rows_ref(cos_ref, r0, QK_ROWS)[...] * gain)
        sin = jnp.where(plain, 0.0, _block_rows_ref(sin_ref, r0, QK_ROWS)[...] * gain_rot)
        cos = jnp.concatenate([cos] * (n // tw), axis=1)
        sin = jnp.concatenate([sin] * (n // tw), axis=1)
        partner = jnp.where(lane_lo, pltpu.roll(y, n - half, 1), pltpu.roll(y, half, 1))
        _block_rows_ref(o_ref, r0, QK_ROWS)[...] = (r * (y * cos + partner * sin)).astype(o_ref.dtype)


def _qkv_kernel(x_ref, mod_ref, g_ref, w_ref, gain_ref, cos_ref, sin_ref, o_ref, h_scr, y_scr, ss_scr,
                *, planes, n_chunks, n_steps):
    t = pl.program_id(0)
    tm = x_ref.shape[0]
    j = jnp.minimum(t, n_steps - 2) % n_chunks
    slot = t % 2

    @pl.when(t == 0)
    def _():
        y_scr[...] = jnp.zeros_like(y_scr)
        ss_scr[...] = jnp.zeros_like(ss_scr)

    @pl.when(jnp.logical_and(j == 0, t < n_steps - 1))
    def _():
        h = _normmod(x_ref[...], g_ref[...], mod_ref[1:2, :], mod_ref[0:1, :])
        if planes == 1:
            h_scr[...] = h.astype(BF16)
        else:
            rows = tm // planes
            for p, hp in enumerate(_to_planes(h, y_scr.at[slot], planes)):
                h_scr[p * rows:(p + 1) * rows, :] = hp.astype(BF16)

    prev_kind = (jnp.maximum(t - 1, 0) % n_chunks) % 3
    ones = _head_ones(LANE_GROUP)

    def step(cur, prev):
        _rms_rope_store(y_scr.at[prev], ss_scr.at[prev], gain_ref, cos_ref, sin_ref, o_ref, prev_kind == 2)
        y = _dot(h_scr[...], w_ref[...])
        n = y.shape[1]
        for c in range(n // LANES):
            y_scr[cur, c] = y[:, c * LANES:(c + 1) * LANES]
        sq = (y * y).astype(BF16)
        for c in range(0, n, LANE_GROUP):
            ss_scr[cur, :, c:c + LANE_GROUP] = _dot(sq[:, c:c + LANE_GROUP], ones)

    pl.when(slot == 0)(lambda: step(0, 1))
    pl.when(slot == 1)(lambda: step(1, 0))


def _qkv_proj(x, mod, g, w, gains, cos_t, sin_t, planes):
    bsz, s, d = x.shape
    n = w.shape[1]
    tn = ATTN_HEADS * HEAD_DIM
    tm = ATTN_TM
    tw = cos_t.shape[-1]
    assert tn == d
    n_tiles, n_chunks = s // tm, n // tn
    total = bsz * n_tiles * n_chunks

    def where(t):
        tile, j = t // n_chunks, t % n_chunks
        return tile // n_tiles, tile % n_tiles, j

    cur = lambda t: where(jnp.minimum(t, total - 1))
    prev = lambda t: where(jnp.maximum(t - 1, 0))
    if planes == 1:
        tok = lambda width: pl.BlockSpec((None, tm, width), lambda t: (prev(t)[0], prev(t)[1], 0))
        out_spec = pl.BlockSpec((None, tm, tn), lambda t: prev(t))
        out_shape = jax.ShapeDtypeStruct((bsz, s, n), BF16)
    else:
        rows = tm // planes
        tok = lambda width: pl.BlockSpec((None, planes, rows, width), lambda t: (prev(t)[0], 0, prev(t)[1], 0))
        out_spec = pl.BlockSpec((None, planes, rows, tn), lambda t: (prev(t)[0], 0, prev(t)[1], prev(t)[2]))
        out_shape = jax.ShapeDtypeStruct((bsz, planes, s // planes, n), BF16)
    return pl.pallas_call(
        functools.partial(_qkv_kernel, planes=planes, n_chunks=n_chunks, n_steps=total + 1),
        grid=(total + 1,),
        in_specs=[
            pl.BlockSpec((None, tm, d), lambda t: (cur(t)[0], cur(t)[1], 0)),
            pl.BlockSpec((None, 6, d), lambda t: (cur(t)[0], 0, 0)),
            pl.BlockSpec((1, d), lambda t: (0, 0)),
            pl.BlockSpec((d, tn), lambda t: (0, cur(t)[2])),
            pl.BlockSpec((None, 1, tn), lambda t: (prev(t)[2] % 3, 0, 0)),
            tok(tw), tok(tw),
        ],
        out_specs=out_spec,
        out_shape=out_shape,
        scratch_shapes=[pltpu.VMEM((tm, d), BF16), pltpu.VMEM((2, tn // LANES, tm, LANES), F32),
                        pltpu.VMEM((2, tm, tn), F32)],
        compiler_params=_cparams(("arbitrary",)),
        name=f"attn_qkv_p{planes}",
    )(x, mod, g.reshape(1, d), w, gains, cos_t, sin_t)


ATTN_STEP_BLOCKS = 2


def _attn_kernel(q_ref, k_ref, v_ref, kp_ref, vp_ref, o_ref, lse_ref, *, interleave):
    n = pl.program_id(2)
    blk = ATTN_BLOCK
    he = ATTN_HEADS * HEAD_DIM
    seg = blk // interleave

    def pos(axis):
        i = lax.broadcasted_iota(jnp.int32, (blk, blk), axis)
        if interleave == 1:
            return i
        return interleave * jnp.bitwise_and(i, seg - 1) + jnp.right_shift(i, seg.bit_length() - 1)

    def block_of(ref, b):
        if len(ref.shape) == 2:
            return ref[b * blk:(b + 1) * blk, :]
        return ref[:, b * seg:(b + 1) * seg, :].reshape(blk, he)

    qpos, kpos = pos(0), pos(1)
    cur_ok = kpos <= qpos
    prev_ok = kpos >= qpos
    first_ok = kpos >= qpos + jnp.where(n > 0, 0, blk)
    pair = 2 * HEAD_DIM
    lane = lax.broadcasted_iota(jnp.int32, (blk, pair), 1)
    first = lane < HEAD_DIM
    zero = jnp.zeros((), BF16)
    scale = jnp.asarray(HEAD_DIM ** -0.5, BF16)
    sl = lambda hp: slice(hp * pair, (hp + 1) * pair)
    n_blocks = ATTN_STEP_BLOCKS

    q = [block_of(q_ref, b) * scale for b in range(n_blocks)]
    kc = [block_of(k_ref, b) for b in range(n_blocks)]
    vc = [block_of(v_ref, b) for b in range(n_blocks)]
    kp = [kp_ref[...].reshape(blk, he)] + kc[:-1]
    vp = [vp_ref[...].reshape(blk, he)] + vc[:-1]
    masks = [jnp.concatenate([cur_ok, first_ok if b == 0 else prev_ok], axis=1) for b in range(n_blocks)]
    ones = jnp.ones((2 * blk, pair), BF16)
    halves = range(ATTN_HEADS // 2)
    k2 = [[jnp.concatenate([kc[b][:, sl(hp)], kp[b][:, sl(hp)]], axis=0) for hp in halves] for b in range(n_blocks)]
    v2 = [[jnp.concatenate([jnp.concatenate([vc[b][:, sl(hp)], vp[b][:, sl(hp)]], axis=0), ones], axis=1)
           for hp in halves] for b in range(n_blocks)]

    units = [(b, hp, sub) for b in range(n_blocks) for hp in halves for sub in range(2)]
    qh = [jnp.where(first if sub == 0 else jnp.logical_not(first), q[b][:, sl(hp)], zero) for b, hp, sub in units]
    sc = [jnp.where(masks[b], _dot_t(x, k2[b][hp]), NEG_INF) for x, (b, hp, _) in zip(qh, units)]
    m = [jnp.max(a, axis=-1, keepdims=True) for a in sc]
    p = [jnp.exp((a - mx).astype(BF16)) for a, mx in zip(sc, m)]
    acc = [_dot(a, v2[b][hp]) for a, (b, hp, _) in zip(p, units)]
    l = [a[:, pair:] for a in acc]
    o = [a[:, :pair] / dd for a, dd in zip(acc, l)]
    assert lse_ref.shape[-1] == pair
    lse_lane = lax.broadcasted_iota(jnp.int32, (blk, pair), 1)
    per_block = ATTN_HEADS
    for b in range(n_blocks):
        ob = o[b * per_block:(b + 1) * per_block]
        out = jnp.concatenate([jnp.where(first, ob[2 * hp], ob[2 * hp + 1]) for hp in range(ATTN_HEADS // 2)], axis=1)
        lse_tile = jnp.zeros((blk, lse_ref.shape[-1]), F32)
        for idx in range(per_block):
            u = b * per_block + idx
            lse_tile = jnp.where(lse_lane == idx, m[u] + jnp.log(l[u]), lse_tile)
        if len(o_ref.shape) == 2:
            o_ref[b * blk:(b + 1) * blk, :] = out.astype(o_ref.dtype)
            lse_ref[b * blk:(b + 1) * blk, :] = lse_tile
        else:
            o_ref[:, b * seg:(b + 1) * seg, :] = out.astype(o_ref.dtype).reshape(interleave, seg, he)
            lse_ref[:, b * seg:(b + 1) * seg, :] = lse_tile.reshape(interleave, seg, lse_ref.shape[-1])


LSE_LANES = 128


def _dilated_attention(qkv, group):
    he = ATTN_HEADS * HEAD_DIM
    blk = ATTN_BLOCK
    nb = ATTN_STEP_BLOCKS
    dilation = DILATED_GROUPS[group][1]
    if group == 0:
        bsz, s, _ = qkv.shape
        grid = (bsz, 1, s // (nb * blk))
        block = lambda rows, width: (None, rows, width)
        imap = lambda col: (lambda b, r, i: (b, i, col))
        imap_prev = lambda col: (lambda b, r, i: (b, jnp.maximum(nb * i - 1, 0), col))
        out_dims = (bsz, s)
        base, interleave, view, seg = 0, 1, qkv, blk
    else:
        bsz, planes, l2, n = qkv.shape
        interleave = planes // dilation
        seg = blk // interleave
        view = qkv.reshape(bsz, interleave, dilation, l2, n)
        grid = (bsz, dilation, l2 // (nb * seg))
        block = lambda rows, width: (None, interleave, None, rows, width)
        imap = lambda col: (lambda b, r, i: (b, 0, r, i, col))
        imap_prev = lambda col: (lambda b, r, i: (b, 0, r, jnp.maximum(nb * i - 1, 0), col))
        out_dims = (bsz, interleave, dilation, l2)
        base = 3 * (group - 1)
    cur = lambda which: pl.BlockSpec(block(nb * seg, he), imap(base + which))
    prev = lambda which: pl.BlockSpec(block(seg, he), imap_prev(base + which))
    o, lse = pl.pallas_call(
        functools.partial(_attn_kernel, interleave=interleave),
        grid=grid,
        in_specs=[cur(0), cur(1), cur(2), prev(1), prev(2)],
        out_specs=[pl.BlockSpec(block(nb * seg, he), imap(0)), pl.BlockSpec(block(nb * seg, LSE_LANES), imap(0))],
        out_shape=[jax.ShapeDtypeStruct(out_dims + (he,), BF16), jax.ShapeDtypeStruct(out_dims + (LSE_LANES,), F32)],
        compiler_params=_cparams(("parallel", "parallel", "parallel")),
        name=f"dilated_attn_g{group}",
    )(view, view, view, view, view)
    if group == 0:
        return o, lse
    return o.reshape(bsz, planes, l2, he), lse.reshape(bsz, planes, l2, LSE_LANES)


def _merge_kernel(o0_ref, o1_ref, o2_ref, l0_ref, l1_ref, l2_ref, e_ref, w_ref, x_ref, mod_ref, out_ref,
                  tok_scr, lse_scr):
    planes, rows, he = o1_ref.shape
    tm = planes * rows
    l0 = jnp.concatenate(_to_planes(l0_ref[...], lse_scr, planes), axis=0)
    l1 = l1_ref[...].reshape(tm, LSE_LANES)
    l2 = l2_ref[...].reshape(tm, LSE_LANES)
    m = jnp.maximum(jnp.maximum(l0, l1), l2)
    e0, e1, e2 = jnp.exp(l0 - m), jnp.exp(l1 - m), jnp.exp(l2 - m)
    inv = 1.0 / (e0 + e1 + e2)
    expand = e_ref[...]
    widen = lambda wgt: _dot(wgt.astype(BF16), expand)
    part = (widen(e1 * inv) * o1_ref[...].reshape(tm, he).astype(F32)
            + widen(e2 * inv) * o2_ref[...].reshape(tm, he).astype(F32))
    w0 = _from_planes(e0 * inv, lse_scr, planes)
    o = widen(w0) * o0_ref[...].astype(F32) + _from_planes(part, tok_scr, planes)
    y = _dot(o.astype(BF16), w_ref[...])
    out_ref[...] = x_ref[...] + mod_ref[2:3, :] * y


def _attn_merge_proj(outs, lses, w_o, x, mod):
    bsz, s, d = x.shape
    he = outs[0].shape[-1]
    tm = ATTN_TM
    rows = tm // PLANES
    head_of_lane = jnp.arange(he, dtype=jnp.int32) // HEAD_DIM
    expand = (jnp.arange(LSE_LANES, dtype=jnp.int32)[:, None] == head_of_lane[None, :]).astype(BF16)
    tok = lambda width: pl.BlockSpec((None, tm, width), lambda b, i: (b, i, 0))
    pm = lambda width: pl.BlockSpec((None, PLANES, rows, width), lambda b, i: (b, 0, i, 0))
    return pl.pallas_call(
        _merge_kernel,
        grid=(bsz, s // tm),
        in_specs=[tok(he), pm(he), pm(he), tok(LSE_LANES), pm(LSE_LANES), pm(LSE_LANES),
                  pl.BlockSpec((LSE_LANES, he), lambda b, i: (0, 0)),
                  pl.BlockSpec((he, d), lambda b, i: (0, 0)),
                  tok(d),
                  pl.BlockSpec((None, 6, d), lambda b, i: (b, 0, 0))],
        out_specs=tok(d),
        out_shape=jax.ShapeDtypeStruct(x.shape, F32),
        scratch_shapes=[pltpu.VMEM((he // LANES, tm, LANES), F32), pltpu.VMEM((LSE_LANES // LANES, tm, LANES), F32)],
        compiler_params=_cparams(("parallel", "parallel")),
        name="attn_merge_proj",
    )(*outs, *lses, expand, w_o, x, mod)


def _attention_layer(x, mod, g, w_qkv, q_gain, k_gain, w_o, cos_t, sin_t, cos_p, sin_p):
    he = ATTN_HEADS * HEAD_DIM
    gains = jnp.stack([jnp.tile(q_gain, ATTN_HEADS), jnp.tile(k_gain, ATTN_HEADS),
                       jnp.ones((he,), F32)]).reshape(3, 1, he)
    qkv0 = _qkv_proj(x, mod, g, w_qkv[:, :3 * he], gains, cos_t, sin_t, 1)
    qkv12 = _qkv_proj(x, mod, g, w_qkv[:, 3 * he:], gains, cos_p, sin_p, PLANES)
    outs, lses = zip(_dilated_attention(qkv0, 0), _dilated_attention(qkv12, 1), _dilated_attention(qkv12, 2))
    return _attn_merge_proj(outs, lses, w_o, x, mod)


def _pw1_glu_kernel(x_ref, mod_ref, g_ref, wa_ref, wb_ref, ba_ref, bb_ref, o_ref, h_scr):
    j = pl.program_id(2)

    @pl.when(j == 0)
    def _():
        h = _normmod(x_ref[...], g_ref[...], mod_ref[1:2, :], mod_ref[0:1, :])
        h_scr[...] = h.astype(BF16)

    h = h_scr[...]
    a = _dot(h, wa_ref[...]) + ba_ref[...]
    b = _dot(h, wb_ref[...]) + bb_ref[...]
    o_ref[...] = a * _sigmoid(b)


def _pw1_glu(x, mod, g, w_pw1, b_pw1):
    bsz, s, d = x.shape
    ch = w_pw1.shape[1] // 2
    tm, tn = 1024, 512
    nj = ch // tn
    b2 = b_pw1.reshape(1, 2 * ch)
    return pl.pallas_call(
        _pw1_glu_kernel,
        grid=(bsz, s // tm, nj),
        in_specs=[
            pl.BlockSpec((None, tm, d), lambda b, i, j: (b, i, 0)),
            pl.BlockSpec((None, 6, d), lambda b, i, j: (b, 0, 0)),
            pl.BlockSpec((1, d), lambda b, i, j: (0, 0)),
            pl.BlockSpec((d, tn), lambda b, i, j: (0, j)),
            pl.BlockSpec((d, tn), lambda b, i, j: (0, j + nj)),
            pl.BlockSpec((1, tn), lambda b, i, j: (0, j)),
            pl.BlockSpec((1, tn), lambda b, i, j: (0, j + nj)),
        ],
        out_specs=pl.BlockSpec((None, tm, tn), lambda b, i, j: (b, i, j)),
        out_shape=jax.ShapeDtypeStruct((bsz, s, ch), F32),
        scratch_shapes=[pltpu.VMEM((tm, d), BF16)],
        compiler_params=_cparams(("parallel", "parallel", "arbitrary")),
        name="conv_pw1_glu",
    )(x, mod, g.reshape(1, d), w_pw1, w_pw1, b2, b2)


CONV_HALO = 32
CONV_ROWS = 16
SUBLANES = 8


def _dwconv_kernel(u_ref, halo_ref, wdw_ref, bdw_ref, lng_ref, lnb_ref, w2_ref, b2_ref, x_ref, mod_ref,
                   o_ref, ubuf, cbuf):
    i = pl.program_id(1)
    tm = u_ref.shape[0]
    span = tm + CONV_HALO - SUBLANES
    halo = halo_ref[...]
    ubuf[0, 0:CONV_HALO, :] = jnp.where(i > 0, halo, jnp.zeros_like(halo))
    ubuf[0, CONV_HALO:, :] = u_ref[...]
    for sft in range(1, SUBLANES):
        ubuf[sft, 0:span, :] = ubuf[0, sft:sft + span, :]
    first = CONV_HALO - (CONV_WIDTH - 1)
    for c in range(tm // CONV_ROWS):
        accs = [jnp.zeros((SUBLANES, u_ref.shape[1]), F32) for _ in range(CONV_ROWS // SUBLANES)]
        for k in range(CONV_WIDTH):
            off = c * CONV_ROWS + first + k
            sft = off % SUBLANES
            w8 = wdw_ref[k * SUBLANES:(k + 1) * SUBLANES, :]
            for a in range(len(accs)):
                lo = off - sft + a * SUBLANES
                accs[a] = accs[a] + w8 * ubuf[sft, lo:lo + SUBLANES, :]
        for a, acc in enumerate(accs):
            cbuf[c * CONV_ROWS + a * SUBLANES:c * CONV_ROWS + (a + 1) * SUBLANES, :] = acc
    u = cbuf[...] + bdw_ref[...]
    mu = jnp.mean(u, axis=-1, keepdims=True)
    uc = u - mu
    var = jnp.mean(uc * uc, axis=-1, keepdims=True)
    v = uc * lax.rsqrt(var + NORM_EPS) * lng_ref[...] + lnb_ref[...]
    z = (v * _sigmoid(v)).astype(BF16)
    y = _dot(z, w2_ref[...]) + b2_ref[...]
    o_ref[...] = x_ref[...] + mod_ref[2:3, :] * y


def _dwconv_ln_pw2(u, w_dw, b_dw, ln_g, ln_b, w_pw2, b_pw2, x, mod):
    bsz, s, d = x.shape
    ch = u.shape[-1]
    tm = 256
    ratio = tm // CONV_HALO
    wrep = jnp.repeat(w_dw, SUBLANES, axis=0)
    row = lambda a: a.reshape(1, -1)
    return pl.pallas_call(
        _dwconv_kernel,
        grid=(bsz, s // tm),
        in_specs=[
            pl.BlockSpec((None, tm, ch), lambda b, i: (b, i, 0)),
            pl.BlockSpec((None, CONV_HALO, ch), lambda b, i: (b, jnp.maximum(i * ratio - 1, 0), 0)),
            pl.BlockSpec(wrep.shape, lambda b, i: (0, 0)),
            pl.BlockSpec((1, ch), lambda b, i: (0, 0)),
            pl.BlockSpec((1, ch), lambda b, i: (0, 0)),
            pl.BlockSpec((1, ch), lambda b, i: (0, 0)),
            pl.BlockSpec((ch, d), lambda b, i: (0, 0)),
            pl.BlockSpec((1, d), lambda b, i: (0, 0)),
            pl.BlockSpec((None, tm, d), lambda b, i: (b, i, 0)),
            pl.BlockSpec((None, 6, d), lambda b, i: (b, 0, 0)),
        ],
        out_specs=pl.BlockSpec((None, tm, d), lambda b, i: (b, i, 0)),
        out_shape=jax.ShapeDtypeStruct(x.shape, F32),
        scratch_shapes=[pltpu.VMEM((SUBLANES, tm + CONV_HALO, ch), F32), pltpu.VMEM((tm, ch), F32)],
        compiler_params=_cparams(("parallel", "parallel")),
        name="conv_dw_ln_pw2",
    )(u, u, wrep, row(b_dw), row(ln_g), row(ln_b), w_pw2, row(b_pw2), x, mod)


def _conv_layer(x, mod, g, w_pw1, b_pw1, w_dw, b_dw, ln_g, ln_b, w_pw2, b_pw2):
    u = _pw1_glu(x, mod, g, w_pw1, b_pw1)
    return _dwconv_ln_pw2(u, w_dw, b_dw, ln_g, ln_b, w_pw2, b_pw2, x, mod)


def _rwkv_proj_kernel(x_ref, halo_ref, mod_ref, g_ref, mu_ref, wr_ref, wk_ref, wv_ref,
                      wa_ref, wb_ref, aa_ref, ab_ref, ga_ref, gb_ref, vec_ref,
                      r_ref, ld_ref, k_ref, v_ref, kk_ref, a_ref, gate_ref):
    i = pl.program_id(1)
    g = g_ref[...]
    scale, shift = mod_ref[1:2, :], mod_ref[0:1, :]
    h = _normmod(x_ref[...], g, scale, shift)
    h_halo = _normmod(halo_ref[...], g, scale, shift)
    last = jnp.where(i > 0, h_halo[7:8, :], jnp.zeros_like(h_halo[7:8, :]))
    rows = lax.broadcasted_iota(jnp.int32, h.shape, 0)
    h_prev = jnp.where(rows == 0, last, pltpu.roll(h, 1, 0))
    xx = h_prev - h

    def mix(m):
        return (h + xx * mu_ref[m:m + 1, :]).astype(BF16)

    w0, a0, k_k, k_a = vec_ref[0:1, :], vec_ref[1:2, :], vec_ref[2:3, :], vec_ref[3:4, :]
    r = _dot(mix(0), wr_ref[...])
    k = _dot(mix(2), wk_ref[...])
    v = _dot(mix(3), wv_ref[...])
    zw = w0 + _dot(jnp.tanh(_dot(mix(1), wa_ref[...])).astype(BF16), wb_ref[...])
    w = -(jnp.maximum(-zw, 0.0) + jnp.log1p(jnp.exp(-jnp.abs(zw)))) - 0.5
    a = _sigmoid(a0 + _dot(_dot(mix(4), aa_ref[...]).astype(BF16), ab_ref[...]))
    gate = _dot(_sigmoid(_dot(mix(5), ga_ref[...])).astype(BF16), gb_ref[...])
    kk = k * k_k
    ones = _head_ones(LANE_GROUP)
    sq = (kk * kk).astype(BF16)
    d = kk.shape[1]
    ss = jnp.concatenate([_dot(sq[:, c:c + LANE_GROUP], ones) for c in range(0, d, LANE_GROUP)], axis=1)
    kk = kk / jnp.maximum(jnp.sqrt(ss), 1e-12)
    r_ref[...] = r.astype(r_ref.dtype)
    ld_ref[...] = -jnp.exp(w)
    k_ref[...] = (k * (1.0 + (a - 1.0) * k_a)).astype(k_ref.dtype)
    v_ref[...] = v.astype(v_ref.dtype)
    kk_ref[...] = kk.astype(kk_ref.dtype)
    a_ref[...] = a.astype(a_ref.dtype)
    gate_ref[...] = gate.astype(gate_ref.dtype)


def _rwkv_proj(x, mod, g, mu, w_r, w_k, w_v, wa, wb, aa, ab, ga, gb, vecs):
    bsz, s, d = x.shape
    tm = 256
    ratio = tm // 8
    full = lambda a: pl.BlockSpec(a.shape, lambda b, i: (0,) * a.ndim)
    tok = pl.BlockSpec((None, tm, d), lambda b, i: (b, i, 0))
    mu8 = jnp.zeros((8, d), F32).at[:6].set(mu)
    weights = [w_r, w_k, w_v, wa, wb, aa, ab, ga, gb, vecs]
    return pl.pallas_call(
        _rwkv_proj_kernel,
        grid=(bsz, s // tm),
        in_specs=[tok,
                  pl.BlockSpec((None, 8, d), lambda b, i: (b, jnp.maximum(i * ratio - 1, 0), 0)),
                  pl.BlockSpec((None, 6, d), lambda b, i: (b, 0, 0)),
                  pl.BlockSpec((1, d), lambda b, i: (0, 0)),
                  full(mu8)] + [full(a) for a in weights],
        out_specs=[tok] * 7,
        out_shape=[jax.ShapeDtypeStruct(x.shape, F32 if i == 1 else BF16) for i in range(7)],
        compiler_params=_cparams(("parallel", "parallel")),
        name="rwkv_proj",
    )(x, x, mod, g.reshape(1, d), mu8, *weights)


def _wkv_kernel(r_ref, ld_ref, k_ref, v_ref, kk_ref, a_ref, gate_ref, vec_ref, o_ref, state):
    t = pl.program_id(1)

    @pl.when(t == 0)
    def _():
        state[...] = jnp.zeros_like(state)

    c = CHUNK
    w = LANE_GROUP
    nh = w // HEAD_DIM
    n_streams = r_ref.shape[0]
    n_chunks = r_ref.shape[1] // c
    n_groups = r_ref.shape[2] // w
    lane_sl = [slice(g * w, (g + 1) * w) for g in range(n_groups)]
    assert c == HEAD_DIM

    lane_head = _head_of((c, w), 1)
    head_masks = [lane_head == h for h in range(nh)]
    row_c = lax.broadcasted_iota(jnp.int32, (c, w), 0)
    col_in = _pos_in_head((c, w), 1)
    strict = col_in < row_c
    incl = col_in <= row_c
    eye = (col_in == row_c).astype(F32)
    tri = (lax.broadcasted_iota(jnp.int32, (c, c), 1) <= lax.broadcasted_iota(jnp.int32, (c, c), 0)).astype(BF16)
    diag_blocks = _head_of((w, w), 0) == _head_of((w, w), 1)
    ones = diag_blocks.astype(BF16)
    zero16 = jnp.zeros((), BF16)

    def bd(m):
        m16 = m.astype(BF16)
        return jnp.concatenate([jnp.where(msk, m16, zero16) for msk in head_masks], axis=0)

    def bd_cols(m):
        m16 = m.astype(BF16)
        return jnp.where(diag_blocks, jnp.concatenate([m16] * nh, axis=0), zero16)

    def mm(a, b):
        return _dot(a.astype(BF16), b.astype(BF16))

    def prefix_sum(x):
        h1 = x.astype(BF16)
        r1 = x - h1.astype(F32)
        h2 = r1.astype(BF16)
        h3 = (r1 - h2.astype(F32)).astype(BF16)
        return _dot(tri, h1) + _dot(tri, h2) + _dot(tri, h3)

    def each(fn, *lists):
        return [fn(*args) for args in zip(*lists)]

    def body(ci, carry):
        rows = pl.ds(pl.multiple_of(ci * c, c), c)
        units = [(s, lane_sl[g]) for s in range(n_streams) for g in range(n_groups)]
        ld = [ld_ref[s, rows, ln] for s, ln in units]
        kk = [kk_ref[s, rows, ln].astype(F32) for s, ln in units]
        v = [v_ref[s, rows, ln].astype(F32) for s, ln in units]

        h1 = each(lambda x: x.astype(BF16), ld)
        r1 = each(lambda x, h: x - h.astype(F32), ld, h1)
        h2 = each(lambda x: x.astype(BF16), r1)
        h3 = each(lambda x, h: (x - h.astype(F32)).astype(BF16), r1, h2)
        c1 = each(lambda h: _dot(tri, h), h1)
        c2 = each(lambda h: _dot(tri, h), h2)
        c3 = each(lambda h: _dot(tri, h), h3)
        cum = each(lambda x, y, z: x + y + z, c1, c2, c3)
        p_in = each(jnp.exp, cum)
        p_inv = each(lambda x: jnp.exp(-x), cum)
        p_end = each(lambda p: p[c - 1:c, :], p_in)
        at = each(lambda q, x, l: -q * jnp.exp(x - l), kk, cum, ld)
        bt = [q * a_ref[s, rows, ln].astype(F32) * pi for (s, ln), q, pi in zip(units, kk, p_inv)]
        kt = [k_ref[s, rows, ln].astype(F32) * pi for (s, ln), pi in zip(units, p_inv)]
        lhs = [jnp.concatenate([x, r_ref[s, rows, ln].astype(F32) * p], axis=0).astype(BF16)
               for (s, ln), x, p in zip(units, at, p_in)]
        bt_bd = each(bd, bt)
        kt_bd = each(bd, kt)
        g_b = each(_dot_t, lhs, bt_bd)
        g_k = each(_dot_t, lhs, kt_bd)
        a_ab = each(lambda g: jnp.where(strict, g[:c], 0.0), g_b)
        a_ak = each(lambda g: jnp.where(strict, g[:c], 0.0).astype(BF16), g_k)
        b_cat = each(lambda gb, gk: jnp.concatenate(
            [jnp.where(incl, gb[c:], 0.0), jnp.where(incl, gk[c:], 0.0)], axis=1).astype(BF16), g_b, g_k)

        x_inv = each(lambda m: eye + m, a_ab)
        pw = each(lambda m: mm(m, bd_cols(m)), a_ab)
        for _ in range(4):
            both = each(lambda x, p: mm(jnp.concatenate([x, p], axis=0), bd_cols(p)), x_inv, pw)
            x_inv = each(lambda x, b: x + b[:c], x_inv, both)
            pw = each(lambda b: b[c:], both)
        x_inv = each(lambda x, p: (x + mm(x, bd_cols(p))).astype(BF16), x_inv, pw)

        v_bd = each(bd, v)
        akv = each(_dot, a_ak, v_bd)
        bk = each(lambda b, k, p: jnp.concatenate([b * p, k * p], axis=0).astype(BF16), bt, kt, p_end)
        rkr = [(r_ref[s, rows, ln].astype(F32) * k_ref[s, rows, ln].astype(F32) * vec_ref[0:1, ln]).astype(BF16)
               for s, ln in units]
        bonus = each(lambda x, val: _dot(x, ones) * val, rkr, v)

        s_bd = [state[s, g] for s in range(n_streams) for g in range(n_groups)]
        sa_sr = each(lambda l, st: _dot_t(l, st.astype(BF16)), lhs, s_bd)
        u_rhs = each(lambda x, y: bd(x[:c] + y), sa_sr, akv)
        u = each(_dot, x_inv, u_rhs)
        uv_t = each(lambda x, val: jnp.concatenate([x, val], axis=0).T.astype(BF16), u, v)
        upd = each(_dot, uv_t, bk)
        s_new = each(lambda st, p, x: st * p + jnp.where(diag_blocks, x, 0.0), s_bd, p_end, upd)
        i = 0
        for s in range(n_streams):
            for g in range(n_groups):
                state[s, g] = s_new[i]
                i += 1
        y_rhs = each(lambda x, vb: jnp.concatenate([bd(x), vb], axis=0), u, v_bd)
        y2 = each(_dot, b_cat, y_rhs)
        y = each(lambda x, z: x[c:] + z, sa_sr, y2)

        inv_n = 1.0 / HEAD_DIM
        y_hi = each(lambda x: x.astype(BF16), y)
        y_lo = each(lambda x, h: (x - h.astype(F32)).astype(BF16), y, y_hi)
        m_hi = each(lambda h: _dot(h, ones), y_hi)
        m_lo = each(lambda h: _dot(h, ones), y_lo)
        yc = each(lambda x, a, b: x - (a + b) * inv_n, y, m_hi, m_lo)
        var = each(lambda x: _dot((x * x).astype(BF16), ones) * inv_n, yc)
        for (s, ln), x, vr, bo in zip(units, yc, var, bonus):
            yn = x * lax.rsqrt(vr + RWKV_GN_EPS) * vec_ref[1:2, ln] + vec_ref[2:3, ln]
            o_ref[s, rows, ln] = ((yn + bo) * gate_ref[s, rows, ln].astype(F32)).astype(o_ref.dtype)
        return carry

    lax.fori_loop(0, n_chunks, body, 0)


WKV_TOKENS = 256


def _wkv_scan(r, ld, k, v, kk, a, gate, vecs):
    bsz, s, d = r.shape
    tb, lw = WKV_TOKENS, d
    tok = pl.BlockSpec((bsz, tb, lw), lambda gi, t: (0, t, gi))
    return pl.pallas_call(
        _wkv_kernel,
        grid=(d // lw, s // tb),
        in_specs=[tok] * 7 + [pl.BlockSpec((8, lw), lambda gi, t: (0, gi))],
        out_specs=tok,
        out_shape=jax.ShapeDtypeStruct((bsz, s, d), BF16),
        scratch_shapes=[pltpu.VMEM((bsz, lw // LANE_GROUP, LANE_GROUP, LANE_GROUP), F32)],
        compiler_params=_cparams(("parallel", "arbitrary")),
        name="wkv7_scan",
    )(r, ld, k, v, kk, a, gate, vecs)


def _rwkv_layer(x, mod, g, mu, w_r, w_k, w_v, w_o, w0, wa, wb, a0, aa, ab, ga, gb, k_k, k_a, r_k, ln_g, ln_b):
    d = x.shape[-1]
    vecs = jnp.zeros((8, d), F32).at[0].set(w0).at[1].set(a0).at[2].set(k_k).at[3].set(k_a)
    r, ld, k, v, kk, a, gate = _rwkv_proj(x, mod, g, mu, w_r, w_k, w_v, wa, wb, aa, ab, ga, gb, vecs)
    svecs = jnp.zeros((8, d), F32).at[0].set(r_k.reshape(d)).at[1].set(ln_g).at[2].set(ln_b)
    z = _wkv_scan(r, ld, k, v, kk, a, gate, svecs)
    return _proj_residual(z, w_o, jnp.zeros((d,), F32), x, mod)


def _rope_tables(positions):
    half = HEAD_DIM // 2
    inv_freq = jnp.power(ROPE_THETA, -jnp.arange(0, HEAD_DIM, 2, dtype=F32) / HEAD_DIM)
    ang = positions.astype(F32)[..., None] * inv_freq
    cos, sin = jnp.cos(ang), jnp.sin(ang)
    assert cos.shape[-1] == half
    return jnp.concatenate([cos, cos, cos, cos], -1), jnp.concatenate([-sin, sin, -sin, sin], -1)


def _plane_major(t):
    bsz, s, n = t.shape
    return t.reshape(bsz, s // PLANES, PLANES, n).transpose(0, 2, 1, 3)


def kernel(x, c, positions, ada_w, ada_b, norm_mix_g, norm_ffn_g, ffn_w_gate, ffn_w_up, ffn_w_down, attn_w_qkv, attn_q_gain, attn_k_gain, attn_w_o, conv_w_pw1, conv_b_pw1, conv_w_dw, conv_b_dw, conv_ln_g, conv_ln_b, conv_w_pw2, conv_b_pw2, rwkv_mu, rwkv_w_r, rwkv_w_k, rwkv_w_v, rwkv_w_o, rwkv_w0, rwkv_w_lora_a, rwkv_w_lora_b, rwkv_a0, rwkv_a_lora_a, rwkv_a_lora_b, rwkv_g_lora_a, rwkv_g_lora_b, rwkv_k_k, rwkv_k_a, rwkv_r_k, rwkv_ln_g, rwkv_ln_b):
    depth = ada_w.shape[0]
    bf = lambda a: a.astype(BF16)
    cos_t, sin_t = _rope_tables(positions)
    cos_p, sin_p = _rope_tables(_plane_major(positions[..., None])[..., 0])
    mods = _ada_mod(c, ada_w, ada_b)
    for i in range(depth):
        mod = mods[i]
        kind, j = i % 3, i // 3
        if kind == 0:
            x = _attention_layer(x, mod, norm_mix_g[i], bf(attn_w_qkv[j]), attn_q_gain[j], attn_k_gain[j],
                                 bf(attn_w_o[j]), cos_t, sin_t, cos_p, sin_p)
        elif kind == 1:
            x = _conv_layer(x, mod, norm_mix_g[i], bf(conv_w_pw1[j]), conv_b_pw1[j], conv_w_dw[j], conv_b_dw[j],
                            conv_ln_g[j], conv_ln_b[j], bf(conv_w_pw2[j]), conv_b_pw2[j])
        else:
            x = _rwkv_layer(x, mod, norm_mix_g[i], rwkv_mu[j], bf(rwkv_w_r[j]), bf(rwkv_w_k[j]), bf(rwkv_w_v[j]),
                            bf(rwkv_w_o[j]), rwkv_w0[j], bf(rwkv_w_lora_a[j]), bf(rwkv_w_lora_b[j]), rwkv_a0[j],
                            bf(rwkv_a_lora_a[j]), bf(rwkv_a_lora_b[j]), bf(rwkv_g_lora_a[j]), bf(rwkv_g_lora_b[j]),
                            rwkv_k_k[j], rwkv_k_a[j], rwkv_r_k[j], rwkv_ln_g[j], rwkv_ln_b[j])
        x = _ffn(x, mod, norm_ffn_g[i], bf(ffn_w_gate[i]), bf(ffn_w_up[i]), bf(ffn_w_down[i]))
    return x
```

```python
import functools

import jax
import jax.numpy as jnp
from jax import lax
from jax.experimental import pallas as pl
from jax.experimental.pallas import tpu as pltpu

F32 = jnp.float32
BF16 = jnp.bfloat16

NORM_EPS = 1e-6
HEAD_DIM = 64
ATTN_HEADS = 16
DILATED_GROUPS = ((128, 1), (512, 4), (2048, 16))
ATTN_BLOCK = 128
ROPE_THETA = 10000.0
NEG_INF = -1e30
CONV_WIDTH = 31
RWKV_GN_EPS = 1e-5 * HEAD_DIM
CHUNK = 64
LANE_GROUP = 256
VMEM_LIMIT = 56 * 1024 * 1024


def _cparams(sem):
    return pltpu.CompilerParams(dimension_semantics=sem, vmem_limit_bytes=VMEM_LIMIT)


def _dot(a, b):
    return jnp.dot(a, b, preferred_element_type=F32)


def _dot_t(a, b):
    return lax.dot_general(a, b, (((1,), (1,)), ((), ())), preferred_element_type=F32)


def _sigmoid(x):
    return 1.0 / (1.0 + jnp.exp(-x))


def _normmod(x, g, scale, shift):
    ms = jnp.mean(x * x, axis=-1, keepdims=True)
    return x * lax.rsqrt(ms + NORM_EPS) * g * (1.0 + scale) + shift


HEAD_SHIFT = HEAD_DIM.bit_length() - 1
assert 1 << HEAD_SHIFT == HEAD_DIM


def _head_of(shape, axis):
    return jnp.right_shift(lax.broadcasted_iota(jnp.int32, shape, axis), HEAD_SHIFT)


def _pos_in_head(shape, axis):
    return jnp.bitwise_and(lax.broadcasted_iota(jnp.int32, shape, axis), HEAD_DIM - 1)


def _head_ones(n):
    return (_head_of((n, n), 0) == _head_of((n, n), 1)).astype(BF16)


def _ada_kernel(c_ref, w_ref, b_ref, o_ref):
    c = c_ref[...]
    c_act = (c * _sigmoid(c)).astype(BF16)
    o_ref[...] = _dot(c_act, w_ref[...].astype(BF16)) + b_ref[...]


def _ada_mod(c, ada_w, ada_b):
    depth, d, n = ada_w.shape
    bsz = c.shape[0]
    rows = 8
    c_pad = jnp.zeros((rows, d), F32).at[:bsz].set(c)
    tn = 1536
    out = pl.pallas_call(
        _ada_kernel,
        grid=(depth, n // tn),
        in_specs=[
            pl.BlockSpec((rows, d), lambda i, j: (0, 0)),
            pl.BlockSpec((None, d, tn), lambda i, j: (i, 0, j)),
            pl.BlockSpec((None, 1, tn), lambda i, j: (i, 0, j)),
        ],
        out_specs=pl.BlockSpec((None, rows, tn), lambda i, j: (i, 0, j)),
        out_shape=jax.ShapeDtypeStruct((depth, rows, n), F32),
        compiler_params=_cparams(("parallel", "parallel")),
        name="ada_mod",
    )(c_pad, ada_w, ada_b.reshape(depth, 1, n))
    return out[:, :bsz].reshape(depth, bsz, 6, d)


FFN_SUB = 256


def _ffn_kernel(x_ref, mod_ref, g_ref, wg_ref, wu_ref, wd_ref, o_ref, h_scr, acc_scr):
    j = pl.program_id(2)

    @pl.when(j == 0)
    def _():
        h = _normmod(x_ref[...], g_ref[...], mod_ref[4:5, :], mod_ref[3:4, :])
        h_scr[...] = h.astype(BF16)
        acc_scr[...] = jnp.zeros_like(acc_scr)

    h = h_scr[...]
    tf = wg_ref.shape[1]
    cols = [(c0, min(c0 + FFN_SUB, tf)) for c0 in range(0, tf, FFN_SUB)]
    gate_up = lambda c: (_dot(h, wg_ref[:, c[0]:c[1]]), _dot(h, wu_ref[:, c[0]:c[1]]))
    nxt = gate_up(cols[0])
    for idx, c in enumerate(cols):
        a, u = nxt
        if idx + 1 < len(cols):
            nxt = gate_up(cols[idx + 1])
        z = (a * _sigmoid(a) * u).astype(BF16)
        acc_scr[...] += _dot(z, wd_ref[c[0]:c[1], :])

    @pl.when(j == pl.num_programs(2) - 1)
    def _():
        o_ref[...] = x_ref[...] + mod_ref[5:6, :] * acc_scr[...]


def _ffn(x, mod, g, wg, wu, wd):
    bsz, s, d = x.shape
    f = wg.shape[1]
    tm, tf = 1024, f // 2
    return pl.pallas_call(
        _ffn_kernel,
        grid=(bsz, s // tm, f // tf),
        in_specs=[
            pl.BlockSpec((None, tm, d), lambda b, i, j: (b, i, 0)),
            pl.BlockSpec((None, 6, d), lambda b, i, j: (b, 0, 0)),
            pl.BlockSpec((1, d), lambda b, i, j: (0, 0)),
            pl.BlockSpec((d, tf), lambda b, i, j: (0, j)),
            pl.BlockSpec((d, tf), lambda b, i, j: (0, j)),
            pl.BlockSpec((tf, d), lambda b, i, j: (j, 0)),
        ],
        out_specs=pl.BlockSpec((None, tm, d), lambda b, i, j: (b, i, 0)),
        out_shape=jax.ShapeDtypeStruct(x.shape, F32),
        scratch_shapes=[pltpu.VMEM((tm, d), BF16), pltpu.VMEM((tm, d), F32)],
        compiler_params=_cparams(("parallel", "parallel", "arbitrary")),
        name="ffn",
    )(x, mod, g.reshape(1, d), wg, wu, wd)


def _proj_res_kernel(z_ref, w_ref, b_ref, x_ref, mod_ref, o_ref):
    y = _dot(z_ref[...], w_ref[...]) + b_ref[...]
    o_ref[...] = x_ref[...] + mod_ref[2:3, :] * y


def _proj_residual(z, w, bias, x, mod):
    bsz, s, d = x.shape
    k = z.shape[-1]
    tm = 1024
    return pl.pallas_call(
        _proj_res_kernel,
        grid=(bsz, s // tm),
        in_specs=[
            pl.BlockSpec((None, tm, k), lambda b, i: (b, i, 0)),
            pl.BlockSpec((k, d), lambda b, i: (0, 0)),
            pl.BlockSpec((1, d), lambda b, i: (0, 0)),
            pl.BlockSpec((None, tm, d), lambda b, i: (b, i, 0)),
            pl.BlockSpec((None, 6, d), lambda b, i: (b, 0, 0)),
        ],
        out_specs=pl.BlockSpec((None, tm, d), lambda b, i: (b, i, 0)),
        out_shape=jax.ShapeDtypeStruct(x.shape, F32),
        compiler_params=_cparams(("parallel", "parallel")),
        name="proj_residual",
    )(z, w, bias.reshape(1, d), x, mod)


PLANES = DILATED_GROUPS[-1][1]
ATTN_TM = 1024


LANES = 128


def _to_planes(val, scr, planes):
    tm, n = val.shape
    rows = tm // planes
    for c in range(n // LANES):
        scr[c] = val[:, c * LANES:(c + 1) * LANES]
    return [jnp.concatenate([scr[c, pl.ds(p, rows, stride=planes), :] for c in range(n // LANES)], axis=1)
            for p in range(planes)]


def _from_planes(val, scr, planes):
    tm, n = val.shape
    rows = tm // planes
    for p in range(planes):
        for c in range(n // LANES):
            scr[c, pl.ds(p, rows, stride=planes), :] = val[p * rows:(p + 1) * rows, c * LANES:(c + 1) * LANES]
    return jnp.concatenate([scr[c] for c in range(n // LANES)], axis=1)


QK_ROWS = 32


def _block_rows_ref(ref, r0, nrows):
    if len(ref.shape) == 2:
        return ref.at[r0:r0 + nrows, :]
    per = ref.shape[1]
    return ref.at[r0 // per, r0 % per:r0 % per + nrows, :]


def _rms_rope_store(y_ref, ss_ref, gain_ref, cos_ref, sin_ref, o_ref, plain):
    tm, n = ss_ref.shape
    tw = cos_ref.shape[-1]
    half = HEAD_DIM // 2
    gain = gain_ref[:, 0:tw]
    gain_rot = pltpu.roll(gain, half, 1)
    lane_lo = _pos_in_head((QK_ROWS, n), 1) < half
    for r0 in range(0, tm, QK_ROWS):
        y = jnp.concatenate([y_ref[c, r0:r0 + QK_ROWS, :] for c in range(n // LANES)], axis=1)
        r = jnp.where(plain, 1.0, lax.rsqrt(ss_ref[r0:r0 + QK_ROWS, :] * (1.0 / HEAD_DIM) + NORM_EPS))
        cos = jnp.where(plain, 1.0, _block_rows_ref(cos_ref, r0, QK_ROWS)[...] * gain)
        sin = jnp.where(plain, 0.0, _block_rows_ref(sin_ref, r0, QK_ROWS)[...] * gain_rot)
        cos = jnp.concatenate([cos] * (n // tw), axis=1)
        sin = jnp.concatenate([sin] * (n // tw), axis=1)
        partner = jnp.where(lane_lo, pltpu.roll(y, n - half, 1), pltpu.roll(y, half, 1))
        _block_rows_ref(o_ref, r0, QK_ROWS)[...] = (r * (y * cos + partner * sin)).astype(o_ref.dtype)


def _qkv_kernel(x_ref, mod_ref, g_ref, w_ref, gain_ref, cos_ref, sin_ref, o_ref, h_scr, y_scr, ss_scr,
                *, planes, n_chunks, n_steps):
    t = pl.program_id(0)
    tm = x_ref.shape[0]
    j = jnp.minimum(t, n_steps - 2) % n_chunks
    slot = t % 2

    @pl.when(t == 0)
    def _():
        y_scr[...] = jnp.zeros_like(y_scr)
        ss_scr[...] = jnp.zeros_like(ss_scr)

    @pl.when(jnp.logical_and(j == 0, t < n_steps - 1))
    def _():
        h = _normmod(x_ref[...], g_ref[...], mod_ref[1:2, :], mod_ref[0:1, :])
        if planes == 1:
            h_scr[...] = h.astype(BF16)
        else:
            rows = tm // planes
            for p, hp in enumerate(_to_planes(h, y_scr.at[slot], planes)):
                h_scr[p * rows:(p + 1) * rows, :] = hp.astype(BF16)

    prev_kind = (jnp.maximum(t - 1, 0) % n_chunks) % 3
    ones = _head_ones(LANE_GROUP)

    def step(cur, prev):
        _rms_rope_store(y_scr.at[prev], ss_scr.at[prev], gain_ref, cos_ref, sin_ref, o_ref, prev_kind == 2)
        y = _dot(h_scr[...], w_ref[...])
        n = y.shape[1]
        for c in range(n // LANES):
            y_scr[cur, c] = y[:, c * LANES:(c + 1) * LANES]
        sq = (y * y).astype(BF16)
        for c in range(0, n, LANE_GROUP):
            ss_scr[cur, :, c:c + LANE_GROUP] = _dot(sq[:, c:c + LANE_GROUP], ones)

    pl.when(slot == 0)(lambda: step(0, 1))
    pl.when(slot == 1)(lambda: step(1, 0))


def _qkv_proj(x, mod, g, w, chunk0, n_chunks, gains, cos_t, sin_t, planes):
    bsz, s, d = x.shape
    tn = ATTN_HEADS * HEAD_DIM
    n = n_chunks * tn
    tm = ATTN_TM
    tw = cos_t.shape[-1]
    assert tn == d
    n_tiles = s // tm
    total = bsz * n_tiles * n_chunks

    def where(t):
        tile, j = t // n_chunks, t % n_chunks
        return tile // n_tiles, tile % n_tiles, j

    cur = lambda t: where(jnp.minimum(t, total - 1))
    prev = lambda t: where(jnp.maximum(t - 1, 0))
    if planes == 1:
        tok = lambda width: pl.BlockSpec((None, tm, width), lambda t: (prev(t)[0], prev(t)[1], 0))
        out_spec = pl.BlockSpec((None, tm, tn), lambda t: prev(t))
        out_shape = jax.ShapeDtypeStruct((bsz, s, n), BF16)
    else:
        rows = tm // planes
        tok = lambda width: pl.BlockSpec((None, planes, rows, width), lambda t: (prev(t)[0], 0, prev(t)[1], 0))
        out_spec = pl.BlockSpec((None, planes, rows, tn), lambda t: (prev(t)[0], 0, prev(t)[1], prev(t)[2]))
        out_shape = jax.ShapeDtypeStruct((bsz, planes, s // planes, n), BF16)
    return pl.pallas_call(
        functools.partial(_qkv_kernel, planes=planes, n_chunks=n_chunks, n_steps=total + 1),
        grid=(total + 1,),
        in_specs=[
            pl.BlockSpec((None, tm, d), lambda t: (cur(t)[0], cur(t)[1], 0)),
            pl.BlockSpec((None, 6, d), lambda t: (cur(t)[0], 0, 0)),
            pl.BlockSpec((1, d), lambda t: (0, 0)),
            pl.BlockSpec((d, tn), lambda t: (0, chunk0 + cur(t)[2])),
            pl.BlockSpec((None, 1, tn), lambda t: (prev(t)[2] % 3, 0, 0)),
            tok(tw), tok(tw),
        ],
        out_specs=out_spec,
        out_shape=out_shape,
        scratch_shapes=[pltpu.VMEM((tm, d), BF16), pltpu.VMEM((2, tn // LANES, tm, LANES), F32),
                        pltpu.VMEM((2, tm, tn), F32)],
        compiler_params=_cparams(("arbitrary",)),
        name=f"attn_qkv_p{planes}",
    )(x, mod, g.reshape(1, d), w, gains, cos_t, sin_t)


ATTN_STEP_BLOCKS = 4


def _attn_kernel(q_ref, k_ref, v_ref, kp_ref, vp_ref, o_ref, lse_ref, *, interleave):
    n = pl.program_id(2)
    blk = ATTN_BLOCK
    he = ATTN_HEADS * HEAD_DIM
    seg = blk // interleave

    def pos(axis):
        i = lax.broadcasted_iota(jnp.int32, (blk, blk), axis)
        if interleave == 1:
            return i
        return interleave * jnp.bitwise_and(i, seg - 1) + jnp.right_shift(i, seg.bit_length() - 1)

    def block_of(ref, b):
        if len(ref.shape) == 2:
            return ref[b * blk:(b + 1) * blk, :]
        return ref[:, b * seg:(b + 1) * seg, :].reshape(blk, he)

    qpos, kpos = pos(0), pos(1)
    cur_ok = kpos <= qpos
    prev_ok = kpos >= qpos
    first_ok = kpos >= qpos + jnp.where(n > 0, 0, blk)
    pair = 2 * HEAD_DIM
    lane = lax.broadcasted_iota(jnp.int32, (blk, pair), 1)
    first = lane < HEAD_DIM
    zero = jnp.zeros((), BF16)
    scale = jnp.asarray(HEAD_DIM ** -0.5, BF16)
    sl = lambda hp: slice(hp * pair, (hp + 1) * pair)
    n_blocks = ATTN_STEP_BLOCKS

    q = [block_of(q_ref, b) * scale for b in range(n_blocks)]
    kc = [block_of(k_ref, b) for b in range(n_blocks)]
    vc = [block_of(v_ref, b) for b in range(n_blocks)]
    kp = [kp_ref[...].reshape(blk, he)] + kc[:-1]
    vp = [vp_ref[...].reshape(blk, he)] + vc[:-1]
    masks = [jnp.concatenate([cur_ok, first_ok if b == 0 else prev_ok], axis=1) for b in range(n_blocks)]
    ones = jnp.ones((2 * blk, pair), BF16)
    halves = range(ATTN_HEADS // 2)
    k2 = [[jnp.concatenate([kc[b][:, sl(hp)], kp[b][:, sl(hp)]], axis=0) for hp in halves] for b in range(n_blocks)]
    v2 = [[jnp.concatenate([jnp.concatenate([vc[b][:, sl(hp)], vp[b][:, sl(hp)]], axis=0), ones], axis=1)
           for hp in halves] for b in range(n_blocks)]

    units = [(b, hp, sub) for b in range(n_blocks) for hp in halves for sub in range(2)]
    qh = [jnp.where(first if sub == 0 else jnp.logical_not(first), q[b][:, sl(hp)], zero) for b, hp, sub in units]
    sc = [jnp.where(masks[b], _dot_t(x, k2[b][hp]), NEG_INF) for x, (b, hp, _) in zip(qh, units)]
    m = [jnp.max(a, axis=-1, keepdims=True) for a in sc]
    p = [jnp.exp((a - mx).astype(BF16)) for a, mx in zip(sc, m)]
    acc = [_dot(a, v2[b][hp]) for a, (b, hp, _) in zip(p, units)]
    l = [a[:, pair:] for a in acc]
    o = [a[:, :pair] / dd for a, dd in zip(acc, l)]
    assert lse_ref.shape[-1] == pair
    lse_lane = lax.broadcasted_iota(jnp.int32, (blk, pair), 1)
    per_block = ATTN_HEADS
    for b in range(n_blocks):
        ob = o[b * per_block:(b + 1) * per_block]
        out = jnp.concatenate([jnp.where(first, ob[2 * hp], ob[2 * hp + 1]) for hp in range(ATTN_HEADS // 2)], axis=1)
        lse_tile = jnp.zeros((blk, lse_ref.shape[-1]), F32)
        for idx in range(per_block):
            u = b * per_block + idx
            lse_tile = jnp.where(lse_lane == idx, m[u] + jnp.log(l[u]), lse_tile)
        if len(o_ref.shape) == 2:
            o_ref[b * blk:(b + 1) * blk, :] = out.astype(o_ref.dtype)
            lse_ref[b * blk:(b + 1) * blk, :] = lse_tile
        else:
            o_ref[:, b * seg:(b + 1) * seg, :] = out.astype(o_ref.dtype).reshape(interleave, seg, he)
            lse_ref[:, b * seg:(b + 1) * seg, :] = lse_tile.reshape(interleave, seg, lse_ref.shape[-1])


LSE_LANES = 128


def _dilated_attention(qkv, group):
    he = ATTN_HEADS * HEAD_DIM
    blk = ATTN_BLOCK
    nb = ATTN_STEP_BLOCKS
    dilation = DILATED_GROUPS[group][1]
    if group == 0:
        bsz, s, _ = qkv.shape
        grid = (bsz, 1, s // (nb * blk))
        block = lambda rows, width: (None, rows, width)
        imap = lambda col: (lambda b, r, i: (b, i, col))
        imap_prev = lambda col: (lambda b, r, i: (b, jnp.maximum(nb * i - 1, 0), col))
        out_dims = (bsz, s)
        base, interleave, view, seg = 0, 1, qkv, blk
    else:
        bsz, planes, l2, n = qkv.shape
        interleave = planes // dilation
        seg = blk // interleave
        view = qkv.reshape(bsz, interleave, dilation, l2, n)
        grid = (bsz, dilation, l2 // (nb * seg))
        block = lambda rows, width: (None, interleave, None, rows, width)
        imap = lambda col: (lambda b, r, i: (b, 0, r, i, col))
        imap_prev = lambda col: (lambda b, r, i: (b, 0, r, jnp.maximum(nb * i - 1, 0), col))
        out_dims = (bsz, interleave, dilation, l2)
        base = 3 * (group - 1)
    cur = lambda which: pl.BlockSpec(block(nb * seg, he), imap(base + which))
    prev = lambda which: pl.BlockSpec(block(seg, he), imap_prev(base + which))
    o, lse = pl.pallas_call(
        functools.partial(_attn_kernel, interleave=interleave),
        grid=grid,
        in_specs=[cur(0), cur(1), cur(2), prev(1), prev(2)],
        out_specs=[pl.BlockSpec(block(nb * seg, he), imap(0)), pl.BlockSpec(block(nb * seg, LSE_LANES), imap(0))],
        out_shape=[jax.ShapeDtypeStruct(out_dims + (he,), BF16), jax.ShapeDtypeStruct(out_dims + (LSE_LANES,), F32)],
        compiler_params=_cparams(("parallel", "parallel", "parallel")),
        name=f"dilated_attn_g{group}",
    )(view, view, view, view, view)
    if group == 0:
        return o, lse
    return o.reshape(bsz, planes, l2, he), lse.reshape(bsz, planes, l2, LSE_LANES)


def _merge_kernel(o0_ref, o1_ref, o2_ref, l0_ref, l1_ref, l2_ref, e_ref, w_ref, x_ref, mod_ref, out_ref,
                  tok_scr, lse_scr):
    planes, rows, he = o1_ref.shape
    tm = planes * rows
    l0 = jnp.concatenate(_to_planes(l0_ref[...], lse_scr, planes), axis=0)
    l1 = l1_ref[...].reshape(tm, LSE_LANES)
    l2 = l2_ref[...].reshape(tm, LSE_LANES)
    m = jnp.maximum(jnp.maximum(l0, l1), l2)
    e0, e1, e2 = jnp.exp(l0 - m), jnp.exp(l1 - m), jnp.exp(l2 - m)
    inv = 1.0 / (e0 + e1 + e2)
    expand = e_ref[...]
    widen = lambda wgt: _dot(wgt.astype(BF16), expand)
    part = (widen(e1 * inv) * o1_ref[...].reshape(tm, he).astype(F32)
            + widen(e2 * inv) * o2_ref[...].reshape(tm, he).astype(F32))
    w0 = _from_planes(e0 * inv, lse_scr, planes)
    o = widen(w0) * o0_ref[...].astype(F32) + _from_planes(part, tok_scr, planes)
    y = _dot(o.astype(BF16), w_ref[...])
    out_ref[...] = x_ref[...] + mod_ref[2:3, :] * y


def _attn_merge_proj(outs, lses, w_o, x, mod):
    bsz, s, d = x.shape
    he = outs[0].shape[-1]
    tm = ATTN_TM
    rows = tm // PLANES
    head_of_lane = jnp.arange(he, dtype=jnp.int32) // HEAD_DIM
    expand = (jnp.arange(LSE_LANES, dtype=jnp.int32)[:, None] == head_of_lane[None, :]).astype(BF16)
    tok = lambda width: pl.BlockSpec((None, tm, width), lambda b, i: (b, i, 0))
    pm = lambda width: pl.BlockSpec((None, PLANES, rows, width), lambda b, i: (b, 0, i, 0))
    return pl.pallas_call(
        _merge_kernel,
        grid=(bsz, s // tm),
        in_specs=[tok(he), pm(he), pm(he), tok(LSE_LANES), pm(LSE_LANES), pm(LSE_LANES),
                  pl.BlockSpec((LSE_LANES, he), lambda b, i: (0, 0)),
                  pl.BlockSpec((he, d), lambda b, i: (0, 0)),
                  tok(d),
                  pl.BlockSpec((None, 6, d), lambda b, i: (b, 0, 0))],
        out_specs=tok(d),
        out_shape=jax.ShapeDtypeStruct(x.shape, F32),
        scratch_shapes=[pltpu.VMEM((he // LANES, tm, LANES), F32), pltpu.VMEM((LSE_LANES // LANES, tm, LANES), F32)],
        compiler_params=_cparams(("parallel", "parallel")),
        name="attn_merge_proj",
    )(*outs, *lses, expand, w_o, x, mod)


def _attention_layer(x, mod, g, w_qkv, q_gain, k_gain, w_o, cos_t, sin_t, cos_p, sin_p):
    he = ATTN_HEADS * HEAD_DIM
    gains = jnp.stack([jnp.tile(q_gain, ATTN_HEADS), jnp.tile(k_gain, ATTN_HEADS),
                       jnp.ones((he,), F32)]).reshape(3, 1, he)
    qkv0 = _qkv_proj(x, mod, g, w_qkv, 0, 3, gains, cos_t, sin_t, 1)
    qkv12 = _qkv_proj(x, mod, g, w_qkv, 3, 6, gains, cos_p, sin_p, PLANES)
    outs, lses = zip(_dilated_attention(qkv0, 0), _dilated_attention(qkv12, 1), _dilated_attention(qkv12, 2))
    return _attn_merge_proj(outs, lses, w_o, x, mod)


def _pw1_glu_kernel(x_ref, mod_ref, g_ref, wa_ref, wb_ref, ba_ref, bb_ref, o_ref, h_scr):
    j = pl.program_id(2)

    @pl.when(j == 0)
    def _():
        h = _normmod(x_ref[...], g_ref[...], mod_ref[1:2, :], mod_ref[0:1, :])
        h_scr[...] = h.astype(BF16)

    h = h_scr[...]
    a = _dot(h, wa_ref[...]) + ba_ref[...]
    b = _dot(h, wb_ref[...]) + bb_ref[...]
    o_ref[...] = a * _sigmoid(b)


def _pw1_glu(x, mod, g, w_pw1, b_pw1):
    bsz, s, d = x.shape
    ch = w_pw1.shape[1] // 2
    tm, tn = 1024, 512
    nj = ch // tn
    b2 = b_pw1.reshape(1, 2 * ch)
    return pl.pallas_call(
        _pw1_glu_kernel,
        grid=(bsz, s // tm, nj),
        in_specs=[
            pl.BlockSpec((None, tm, d), lambda b, i, j: (b, i, 0)),
            pl.BlockSpec((None, 6, d), lambda b, i, j: (b, 0, 0)),
            pl.BlockSpec((1, d), lambda b, i, j: (0, 0)),
            pl.BlockSpec((d, tn), lambda b, i, j: (0, j)),
            pl.BlockSpec((d, tn), lambda b, i, j: (0, j + nj)),
            pl.BlockSpec((1, tn), lambda b, i, j: (0, j)),
            pl.BlockSpec((1, tn), lambda b, i, j: (0, j + nj)),
        ],
        out_specs=pl.BlockSpec((None, tm, tn), lambda b, i, j: (b, i, j)),
        out_shape=jax.ShapeDtypeStruct((bsz, s, ch), F32),
        scratch_shapes=[pltpu.VMEM((tm, d), BF16)],
        compiler_params=_cparams(("parallel", "parallel", "arbitrary")),
        name="conv_pw1_glu",
    )(x, mod, g.reshape(1, d), w_pw1, w_pw1, b2, b2)


CONV_HALO = 32
CONV_ROWS = 16
SUBLANES = 8


def _dwconv_kernel(u_ref, halo_ref, wdw_ref, bdw_ref, lng_ref, lnb_ref, w2_ref, b2_ref, x_ref, mod_ref,
                   o_ref, ubuf, cbuf):
    i = pl.program_id(1)
    tm = u_ref.shape[0]
    span = tm + CONV_HALO - SUBLANES
    halo = halo_ref[...]
    ubuf[0, 0:CONV_HALO, :] = jnp.where(i > 0, halo, jnp.zeros_like(halo))
    ubuf[0, CONV_HALO:, :] = u_ref[...]
    for sft in range(1, SUBLANES):
        ubuf[sft, 0:span, :] = ubuf[0, sft:sft + span, :]
    first = CONV_HALO - (CONV_WIDTH - 1)
    for c in range(tm // CONV_ROWS):
        accs = [jnp.zeros((SUBLANES, u_ref.shape[1]), F32) for _ in range(CONV_ROWS // SUBLANES)]
        for k in range(CONV_WIDTH):
            off = c * CONV_ROWS + first + k
            sft = off % SUBLANES
            w8 = wdw_ref[k * SUBLANES:(k + 1) * SUBLANES, :]
            for a in range(len(accs)):
                lo = off - sft + a * SUBLANES
                accs[a] = accs[a] + w8 * ubuf[sft, lo:lo + SUBLANES, :]
        for a, acc in enumerate(accs):
            cbuf[c * CONV_ROWS + a * SUBLANES:c * CONV_ROWS + (a + 1) * SUBLANES, :] = acc
    u = cbuf[...] + bdw_ref[...]
    mu = jnp.mean(u, axis=-1, keepdims=True)
    uc = u - mu
    var = jnp.mean(uc * uc, axis=-1, keepdims=True)
    v = uc * lax.rsqrt(var + NORM_EPS) * lng_ref[...] + lnb_ref[...]
    z = (v * _sigmoid(v)).astype(BF16)
    y = _dot(z, w2_ref[...]) + b2_ref[...]
    o_ref[...] = x_ref[...] + mod_ref[2:3, :] * y


def _dwconv_ln_pw2(u, w_dw, b_dw, ln_g, ln_b, w_pw2, b_pw2, x, mod):
    bsz, s, d = x.shape
    ch = u.shape[-1]
    tm = 256
    ratio = tm // CONV_HALO
    wrep = jnp.repeat(w_dw, SUBLANES, axis=0)
    row = lambda a: a.reshape(1, -1)
    return pl.pallas_call(
        _dwconv_kernel,
        grid=(bsz, s // tm),
        in_specs=[
            pl.BlockSpec((None, tm, ch), lambda b, i: (b, i, 0)),
            pl.BlockSpec((None, CONV_HALO, ch), lambda b, i: (b, jnp.maximum(i * ratio - 1, 0), 0)),
            pl.BlockSpec(wrep.shape, lambda b, i: (0, 0)),
            pl.BlockSpec((1, ch), lambda b, i: (0, 0)),
            pl.BlockSpec((1, ch), lambda b, i: (0, 0)),
            pl.BlockSpec((1, ch), lambda b, i: (0, 0)),
            pl.BlockSpec((ch, d), lambda b, i: (0, 0)),
            pl.BlockSpec((1, d), lambda b, i: (0, 0)),
            pl.BlockSpec((None, tm, d), lambda b, i: (b, i, 0)),
            pl.BlockSpec((None, 6, d), lambda b, i: (b, 0, 0)),
        ],
        out_specs=pl.BlockSpec((None, tm, d), lambda b, i: (b, i, 0)),
        out_shape=jax.ShapeDtypeStruct(x.shape, F32),
        scratch_shapes=[pltpu.VMEM((SUBLANES, tm + CONV_HALO, ch), F32), pltpu.VMEM((tm, ch), F32)],
        compiler_params=_cparams(("parallel", "parallel")),
        name="conv_dw_ln_pw2",
    )(u, u, wrep, row(b_dw), row(ln_g), row(ln_b), w_pw2, row(b_pw2), x, mod)


def _conv_layer(x, mod, g, w_pw1, b_pw1, w_dw, b_dw, ln_g, ln_b, w_pw2, b_pw2):
    u = _pw1_glu(x, mod, g, w_pw1, b_pw1)
    return _dwconv_ln_pw2(u, w_dw, b_dw, ln_g, ln_b, w_pw2, b_pw2, x, mod)


def _rwkv_proj_kernel(x_ref, halo_ref, mod_ref, g_ref, mu_ref, wr_ref, wk_ref, wv_ref,
                      wa_ref, wb_ref, aa_ref, ab_ref, ga_ref, gb_ref, vec_ref,
                      r_ref, ld_ref, k_ref, v_ref, kk_ref, a_ref, gate_ref):
    i = pl.program_id(1)
    g = g_ref[...]
    scale, shift = mod_ref[1:2, :], mod_ref[0:1, :]
    h = _normmod(x_ref[...], g, scale, shift)
    h_halo = _normmod(halo_ref[...], g, scale, shift)
    last = jnp.where(i > 0, h_halo[7:8, :], jnp.zeros_like(h_halo[7:8, :]))
    rows = lax.broadcasted_iota(jnp.int32, h.shape, 0)
    h_prev = jnp.where(rows == 0, last, pltpu.roll(h, 1, 0))
    xx = h_prev - h

    def mix(m):
        return (h + xx * mu_ref[m:m + 1, :]).astype(BF16)

    w0, a0, k_k, k_a = vec_ref[0:1, :], vec_ref[1:2, :], vec_ref[2:3, :], vec_ref[3:4, :]
    r = _dot(mix(0), wr_ref[...])
    k = _dot(mix(2), wk_ref[...])
    v = _dot(mix(3), wv_ref[...])
    zw = w0 + _dot(jnp.tanh(_dot(mix(1), wa_ref[...])).astype(BF16), wb_ref[...])
    w = -(jnp.maximum(-zw, 0.0) + jnp.log1p(jnp.exp(-jnp.abs(zw)))) - 0.5
    a = _sigmoid(a0 + _dot(_dot(mix(4), aa_ref[...]).astype(BF16), ab_ref[...]))
    gate = _dot(_sigmoid(_dot(mix(5), ga_ref[...])).astype(BF16), gb_ref[...])
    kk = k * k_k
    ones = _head_ones(LANE_GROUP)
    sq = (kk * kk).astype(BF16)
    d = kk.shape[1]
    ss = jnp.concatenate([_dot(sq[:, c:c + LANE_GROUP], ones) for c in range(0, d, LANE_GROUP)], axis=1)
    kk = kk / jnp.maximum(jnp.sqrt(ss), 1e-12)
    r_ref[...] = r.astype(r_ref.dtype)
    ld_ref[...] = -jnp.exp(w)
    k_ref[...] = (k * (1.0 + (a - 1.0) * k_a)).astype(k_ref.dtype)
    v_ref[...] = v.astype(v_ref.dtype)
    kk_ref[...] = kk.astype(kk_ref.dtype)
    a_ref[...] = a.astype(a_ref.dtype)
    gate_ref[...] = gate.astype(gate_ref.dtype)


def _rwkv_proj(x, mod, g, mu, w_r, w_k, w_v, wa, wb, aa, ab, ga, gb, vecs):
    bsz, s, d = x.shape
    tm = 256
    ratio = tm // 8
    full = lambda a: pl.BlockSpec(a.shape, lambda b, i: (0,) * a.ndim)
    tok = pl.BlockSpec((None, tm, d), lambda b, i: (b, i, 0))
    mu8 = jnp.zeros((8, d), F32).at[:6].set(mu)
    weights = [w_r, w_k, w_v, wa, wb, aa, ab, ga, gb, vecs]
    return pl.pallas_call(
        _rwkv_proj_kernel,
        grid=(bsz, s // tm),
        in_specs=[tok,
                  pl.BlockSpec((None, 8, d), lambda b, i: (b, jnp.maximum(i * ratio - 1, 0), 0)),
                  pl.BlockSpec((None, 6, d), lambda b, i: (b, 0, 0)),
                  pl.BlockSpec((1, d), lambda b, i: (0, 0)),
                  full(mu8)] + [full(a) for a in weights],
        out_specs=[tok] * 7,
        out_shape=[jax.ShapeDtypeStruct(x.shape, F32 if i == 1 else BF16) for i in range(7)],
        compiler_params=_cparams(("parallel", "parallel")),
        name="rwkv_proj",
    )(x, x, mod, g.reshape(1, d), mu8, *weights)


def _wkv_kernel(r_ref, ld_ref, k_ref, v_ref, kk_ref, a_ref, gate_ref, vec_ref, o_ref, state):
    t = pl.program_id(1)

    @pl.when(t == 0)
    def _():
        state[...] = jnp.zeros_like(state)

    c = CHUNK
    w = LANE_GROUP
    nh = w // HEAD_DIM
    n_streams = r_ref.shape[0]
    n_chunks = r_ref.shape[1] // c
    n_groups = r_ref.shape[2] // w
    lane_sl = [slice(g * w, (g + 1) * w) for g in range(n_groups)]
    assert c == HEAD_DIM

    lane_head = _head_of((c, w), 1)
    head_masks = [lane_head == h for h in range(nh)]
    row_c = lax.broadcasted_iota(jnp.int32, (c, w), 0)
    col_in = _pos_in_head((c, w), 1)
    strict = col_in < row_c
    incl = col_in <= row_c
    eye = (col_in == row_c).astype(F32)
    tri = (lax.broadcasted_iota(jnp.int32, (c, c), 1) <= lax.broadcasted_iota(jnp.int32, (c, c), 0)).astype(BF16)
    diag_blocks = _head_of((w, w), 0) == _head_of((w, w), 1)
    ones = diag_blocks.astype(BF16)
    zero16 = jnp.zeros((), BF16)

    def bd(m):
        m16 = m.astype(BF16)
        return jnp.concatenate([jnp.where(msk, m16, zero16) for msk in head_masks], axis=0)

    def bd_cols(m):
        m16 = m.astype(BF16)
        return jnp.where(diag_blocks, jnp.concatenate([m16] * nh, axis=0), zero16)

    def mm(a, b):
        return _dot(a.astype(BF16), b.astype(BF16))

    def prefix_sum(x):
        h1 = x.astype(BF16)
        r1 = x - h1.astype(F32)
        h2 = r1.astype(BF16)
        h3 = (r1 - h2.astype(F32)).astype(BF16)
        return _dot(tri, h1) + _dot(tri, h2) + _dot(tri, h3)

    def each(fn, *lists):
        return [fn(*args) for args in zip(*lists)]

    def body(ci, carry):
        rows = pl.ds(pl.multiple_of(ci * c, c), c)
        units = [(s, lane_sl[g]) for s in range(n_streams) for g in range(n_groups)]
        ld = [ld_ref[s, rows, ln] for s, ln in units]
        kk = [kk_ref[s, rows, ln].astype(F32) for s, ln in units]
        v = [v_ref[s, rows, ln].astype(F32) for s, ln in units]

        h1 = each(lambda x: x.astype(BF16), ld)
        r1 = each(lambda x, h: x - h.astype(F32), ld, h1)
        h2 = each(lambda x: x.astype(BF16), r1)
        h3 = each(lambda x, h: (x - h.astype(F32)).astype(BF16), r1, h2)
        c1 = each(lambda h: _dot(tri, h), h1)
        c2 = each(lambda h: _dot(tri, h), h2)
        c3 = each(lambda h: _dot(tri, h), h3)
        cum = each(lambda x, y, z: x + y + z, c1, c2, c3)
        p_in = each(jnp.exp, cum)
        p_inv = each(lambda x: jnp.exp(-x), cum)
        p_end = each(lambda p: p[c - 1:c, :], p_in)
        at = each(lambda q, x, l: -q * jnp.exp(x - l), kk, cum, ld)
        bt = [q * a_ref[s, rows, ln].astype(F32) * pi for (s, ln), q, pi in zip(units, kk, p_inv)]
        kt = [k_ref[s, rows, ln].astype(F32) * pi for (s, ln), pi in zip(units, p_inv)]
        lhs = [jnp.concatenate([x, r_ref[s, rows, ln].astype(F32) * p], axis=0).astype(BF16)
               for (s, ln), x, p in zip(units, at, p_in)]
        bt_bd = each(bd, bt)
        kt_bd = each(bd, kt)
        g_b = each(_dot_t, lhs, bt_bd)
        g_k = each(_dot_t, lhs, kt_bd)
        a_ab = each(lambda g: jnp.where(strict, g[:c], 0.0), g_b)
        a_ak = each(lambda g: jnp.where(strict, g[:c], 0.0).astype(BF16), g_k)
        b_cat = each(lambda gb, gk: jnp.concatenate(
            [jnp.where(incl, gb[c:], 0.0), jnp.where(incl, gk[c:], 0.0)], axis=1).astype(BF16), g_b, g_k)

        x_inv = each(lambda m: eye + m, a_ab)
        pw = each(lambda m: mm(m, bd_cols(m)), a_ab)
        for _ in range(4):
            both = each(lambda x, p: mm(jnp.concatenate([x, p], axis=0), bd_cols(p)), x_inv, pw)
            x_inv = each(lambda x, b: x + b[:c], x_inv, both)
            pw = each(lambda b: b[c:], both)
        x_inv = each(lambda x, p: (x + mm(x, bd_cols(p))).astype(BF16), x_inv, pw)

        v_bd = each(bd, v)
        akv = each(_dot, a_ak, v_bd)
        bk = each(lambda b, k, p: jnp.concatenate([b * p, k * p], axis=0).astype(BF16), bt, kt, p_end)
        rkr = [(r_ref[s, rows, ln].astype(F32) * k_ref[s, rows, ln].astype(F32) * vec_ref[0:1, ln]).astype(BF16)
               for s, ln in units]
        bonus = each(lambda x, val: _dot(x, ones) * val, rkr, v)

        s_bd = [state[s, g] for s in range(n_streams) for g in range(n_groups)]
        sa_sr = each(lambda l, st: _dot_t(l, st.astype(BF16)), lhs, s_bd)
        u_rhs = each(lambda x, y: bd(x[:c] + y), sa_sr, akv)
        u = each(_dot, x_inv, u_rhs)
        uv_t = each(lambda x, val: jnp.concatenate([x, val], axis=0).T.astype(BF16), u, v)
        upd = each(_dot, uv_t, bk)
        s_new = each(lambda st, p, x: st * p + jnp.where(diag_blocks, x, 0.0), s_bd, p_end, upd)
        i = 0
        for s in range(n_streams):
            for g in range(n_groups):
                state[s, g] = s_new[i]
                i += 1
        y_rhs = each(lambda x, vb: jnp.concatenate([bd(x), vb], axis=0), u, v_bd)
        y2 = each(_dot, b_cat, y_rhs)
        y = each(lambda x, z: x[c:] + z, sa_sr, y2)

        inv_n = 1.0 / HEAD_DIM
        y_hi = each(lambda x: x.astype(BF16), y)
        y_lo = each(lambda x, h: (x - h.astype(F32)).astype(BF16), y, y_hi)
        m_hi = each(lambda h: _dot(h, ones), y_hi)
        m_lo = each(lambda h: _dot(h, ones), y_lo)
        yc = each(lambda x, a, b: x - (a + b) * inv_n, y, m_hi, m_lo)
        var = each(lambda x: _dot((x * x).astype(BF16), ones) * inv_n, yc)
        for (s, ln), x, vr, bo in zip(units, yc, var, bonus):
            yn = x * lax.rsqrt(vr + RWKV_GN_EPS) * vec_ref[1:2, ln] + vec_ref[2:3, ln]
            o_ref[s, rows, ln] = ((yn + bo) * gate_ref[s, rows, ln].astype(F32)).astype(o_ref.dtype)
        return carry

    lax.fori_loop(0, n_chunks, body, 0)


WKV_TOKENS = 256


def _wkv_scan(r, ld, k, v, kk, a, gate, vecs):
    bsz, s, d = r.shape
    tb, lw = WKV_TOKENS, d
    tok = pl.BlockSpec((bsz, tb, lw), lambda gi, t: (0, t, gi))
    return pl.pallas_call(
        _wkv_kernel,
        grid=(d // lw, s // tb),
        in_specs=[tok] * 7 + [pl.BlockSpec((8, lw), lambda gi, t: (0, gi))],
        out_specs=tok,
        out_shape=jax.ShapeDtypeStruct((bsz, s, d), BF16),
        scratch_shapes=[pltpu.VMEM((bsz, lw // LANE_GROUP, LANE_GROUP, LANE_GROUP), F32)],
        compiler_params=_cparams(("parallel", "arbitrary")),
        name="wkv7_scan",
    )(r, ld, k, v, kk, a, gate, vecs)


def _rwkv_layer(x, mod, g, mu, w_r, w_k, w_v, w_o, w0, wa, wb, a0, aa, ab, ga, gb, k_k, k_a, r_k, ln_g, ln_b):
    d = x.shape[-1]
    vecs = jnp.zeros((8, d), F32).at[0].set(w0).at[1].set(a0).at[2].set(k_k).at[3].set(k_a)
    r, ld, k, v, kk, a, gate = _rwkv_proj(x, mod, g, mu, w_r, w_k, w_v, wa, wb, aa, ab, ga, gb, vecs)
    svecs = jnp.zeros((8, d), F32).at[0].set(r_k.reshape(d)).at[1].set(ln_g).at[2].set(ln_b)
    z = _wkv_scan(r, ld, k, v, kk, a, gate, svecs)
    return _proj_residual(z, w_o, jnp.zeros((d,), F32), x, mod)


def _rope_tables(positions):
    bsz, s = positions.shape
    half = HEAD_DIM // 2
    inv_freq = jnp.power(ROPE_THETA, -jnp.arange(0, HEAD_DIM, 2, dtype=F32) / HEAD_DIM)
    ang = (positions.astype(F32)[..., None] * inv_freq).reshape(bsz, s * half // LANES, LANES)
    cos, sin = lax.optimization_barrier((jnp.cos(ang), jnp.sin(ang)))
    cos, sin = cos.reshape(bsz, s, half), sin.reshape(bsz, s, half)
    widen = lambda c, sn: (jnp.concatenate([c, c, c, c], -1), jnp.concatenate([-sn, sn, -sn, sn], -1))
    return widen(cos, sin), widen(_plane_major(cos), _plane_major(sin))


def _plane_major(t):
    bsz, s, n = t.shape
    return t.reshape(bsz, s // PLANES, PLANES, n).transpose(0, 2, 1, 3)


def kernel(x, c, positions, ada_w, ada_b, norm_mix_g, norm_ffn_g, ffn_w_gate, ffn_w_up, ffn_w_down, attn_w_qkv, attn_q_gain, attn_k_gain, attn_w_o, conv_w_pw1, conv_b_pw1, conv_w_dw, conv_b_dw, conv_ln_g, conv_ln_b, conv_w_pw2, conv_b_pw2, rwkv_mu, rwkv_w_r, rwkv_w_k, rwkv_w_v, rwkv_w_o, rwkv_w0, rwkv_w_lora_a, rwkv_w_lora_b, rwkv_a0, rwkv_a_lora_a, rwkv_a_lora_b, rwkv_g_lora_a, rwkv_g_lora_b, rwkv_k_k, rwkv_k_a, rwkv_r_k, rwkv_ln_g, rwkv_ln_b):
    depth = ada_w.shape[0]
    bf = lambda a: a.astype(BF16)
    (cos_t, sin_t), (cos_p, sin_p) = _rope_tables(positions)
    mods = _ada_mod(c, ada_w, ada_b)
    for i in range(depth):
        mod = mods[i]
        kind, j = i % 3, i // 3
        if kind == 0:
            x = _attention_layer(x, mod, norm_mix_g[i], bf(attn_w_qkv[j]), attn_q_gain[j], attn_k_gain[j],
                                 bf(attn_w_o[j]), cos_t, sin_t, cos_p, sin_p)
        elif kind == 1:
            x = _conv_layer(x, mod, norm_mix_g[i], bf(conv_w_pw1[j]), conv_b_pw1[j], conv_w_dw[j], conv_b_dw[j],
                            conv_ln_g[j], conv_ln_b[j], bf(conv_w_pw2[j]), conv_b_pw2[j])
        else:
            x = _rwkv_layer(x, mod, norm_mix_g[i], rwkv_mu[j], bf(rwkv_w_r[j]), bf(rwkv_w_k[j]), bf(rwkv_w_v[j]),
                            bf(rwkv_w_o[j]), rwkv_w0[j], bf(rwkv_w_lora_a[j]), bf(rwkv_w_lora_b[j]), rwkv_a0[j],
                            bf(rwkv_a_lora_a[j]), bf(rwkv_a_lora_b[j]), bf(rwkv_g_lora_a[j]), bf(rwkv_g_lora_b[j]),
                            rwkv_k_k[j], rwkv_k_a[j], rwkv_r_k[j], rwkv_ln_g[j], rwkv_ln_b[j])
        x = _ffn(x, mod, norm_ffn_g[i], bf(ffn_w_gate[i]), bf(ffn_w_up[i]), bf(ffn_w_down[i]))
    return x
```

```python
import functools

import jax
import jax.numpy as jnp
from jax import lax
from jax.experimental import pallas as pl
from jax.experimental.pallas import tpu as pltpu

F32 = jnp.float32
BF16 = jnp.bfloat16

NORM_EPS = 1e-6
HEAD_DIM = 64
ATTN_HEADS = 16
DILATED_GROUPS = ((128, 1), (512, 4), (2048, 16))
ATTN_BLOCK = 128
ROPE_THETA = 10000.0
NEG_INF = -1e30
CONV_WIDTH = 31
RWKV_GN_EPS = 1e-5 * HEAD_DIM
CHUNK = 64
LANE_GROUP = 256
VMEM_LIMIT = 56 * 1024 * 1024


def _cparams(sem):
    return pltpu.CompilerParams(dimension_semantics=sem, vmem_limit_bytes=VMEM_LIMIT)


def _dot(a, b):
    return jnp.dot(a, b, preferred_element_type=F32)


def _dot_t(a, b):
    return lax.dot_general(a, b, (((1,), (1,)), ((), ())), preferred_element_type=F32)


def _sigmoid(x):
    return 1.0 / (1.0 + jnp.exp(-x))


def _normmod(x, g, scale, shift):
    ms = jnp.mean(x * x, axis=-1, keepdims=True)
    return x * lax.rsqrt(ms + NORM_EPS) * g * (1.0 + scale) + shift


HEAD_SHIFT = HEAD_DIM.bit_length() - 1
assert 1 << HEAD_SHIFT == HEAD_DIM


def _head_of(shape, axis):
    return jnp.right_shift(lax.broadcasted_iota(jnp.int32, shape, axis), HEAD_SHIFT)


def _pos_in_head(shape, axis):
    return jnp.bitwise_and(lax.broadcasted_iota(jnp.int32, shape, axis), HEAD_DIM - 1)


def _head_ones(n):
    return (_head_of((n, n), 0) == _head_of((n, n), 1)).astype(BF16)


def _ada_kernel(c_ref, w_ref, b_ref, o_ref):
    c = c_ref[...]
    c_act = (c * _sigmoid(c)).astype(BF16)
    o_ref[...] = _dot(c_act, w_ref[...].astype(BF16)) + b_ref[...]


def _ada_mod(c, ada_w, ada_b):
    depth, d, n = ada_w.shape
    bsz = c.shape[0]
    rows = 8
    c_pad = jnp.zeros((rows, d), F32).at[:bsz].set(c)
    tn = 1536
    out = pl.pallas_call(
        _ada_kernel,
        grid=(depth, n // tn),
        in_specs=[
            pl.BlockSpec((rows, d), lambda i, j: (0, 0)),
            pl.BlockSpec((None, d, tn), lambda i, j: (i, 0, j)),
            pl.BlockSpec((None, 1, tn), lambda i, j: (i, 0, j)),
        ],
        out_specs=pl.BlockSpec((None, rows, tn), lambda i, j: (i, 0, j)),
        out_shape=jax.ShapeDtypeStruct((depth, rows, n), F32),
        compiler_params=_cparams(("parallel", "parallel")),
        name="ada_mod",
    )(c_pad, ada_w, ada_b.reshape(depth, 1, n))
    return out[:, :bsz].reshape(depth, bsz, 6, d)


FFN_SUB = 256


def _ffn_kernel(x_ref, mod_ref, g_ref, wg_ref, wu_ref, wd_ref, o_ref, h_scr, acc_scr):
    j = pl.program_id(2)

    @pl.when(j == 0)
    def _():
        h = _normmod(x_ref[...], g_ref[...], mod_ref[4:5, :], mod_ref[3:4, :])
        h_scr[...] = h.astype(BF16)
        acc_scr[...] = jnp.zeros_like(acc_scr)

    h = h_scr[...]
    tf = wg_ref.shape[1]
    cols = [(c0, min(c0 + FFN_SUB, tf)) for c0 in range(0, tf, FFN_SUB)]
    gate_up = lambda c: (_dot(h, wg_ref[:, c[0]:c[1]]), _dot(h, wu_ref[:, c[0]:c[1]]))
    nxt = gate_up(cols[0])
    for idx, c in enumerate(cols):
        a, u = nxt
        if idx + 1 < len(cols):
            nxt = gate_up(cols[idx + 1])
        z = (a * _sigmoid(a) * u).astype(BF16)
        acc_scr[...] += _dot(z, wd_ref[c[0]:c[1], :])

    @pl.when(j == pl.num_programs(2) - 1)
    def _():
        o_ref[...] = x_ref[...] + mod_ref[5:6, :] * acc_scr[...]


def _ffn(x, mod, g, wg, wu, wd):
    bsz, s, d = x.shape
    f = wg.shape[1]
    tm, tf = 1024, f // 2
    return pl.pallas_call(
        _ffn_kernel,
        grid=(bsz, s // tm, f // tf),
        in_specs=[
            pl.BlockSpec((None, tm, d), lambda b, i, j: (b, i, 0)),
            pl.BlockSpec((None, 6, d), lambda b, i, j: (b, 0, 0)),
            pl.BlockSpec((1, d), lambda b, i, j: (0, 0)),
            pl.BlockSpec((d, tf), lambda b, i, j: (0, j)),
            pl.BlockSpec((d, tf), lambda b, i, j: (0, j)),
            pl.BlockSpec((tf, d), lambda b, i, j: (j, 0)),
        ],
        out_specs=pl.BlockSpec((None, tm, d), lambda b, i, j: (b, i, 0)),
        out_shape=jax.ShapeDtypeStruct(x.shape, F32),
        scratch_shapes=[pltpu.VMEM((tm, d), BF16), pltpu.VMEM((tm, d), F32)],
        compiler_params=_cparams(("parallel", "parallel", "arbitrary")),
        name="ffn",
    )(x, mod, g.reshape(1, d), wg, wu, wd)


def _proj_res_kernel(z_ref, w_ref, b_ref, x_ref, mod_ref, o_ref):
    y = _dot(z_ref[...], w_ref[...]) + b_ref[...]
    o_ref[...] = x_ref[...] + mod_ref[2:3, :] * y


def _proj_residual(z, w, bias, x, mod):
    bsz, s, d = x.shape
    k = z.shape[-1]
    tm = 1024
    return pl.pallas_call(
        _proj_res_kernel,
        grid=(bsz, s // tm),
        in_specs=[
            pl.BlockSpec((None, tm, k), lambda b, i: (b, i, 0)),
            pl.BlockSpec((k, d), lambda b, i: (0, 0)),
            pl.BlockSpec((1, d), lambda b, i: (0, 0)),
            pl.BlockSpec((None, tm, d), lambda b, i: (b, i, 0)),
            pl.BlockSpec((None, 6, d), lambda b, i: (b, 0, 0)),
        ],
        out_specs=pl.BlockSpec((None, tm, d), lambda b, i: (b, i, 0)),
        out_shape=jax.ShapeDtypeStruct(x.shape, F32),
        compiler_params=_cparams(("parallel", "parallel")),
        name="proj_residual",
    )(z, w, bias.reshape(1, d), x, mod)


PLANES = DILATED_GROUPS[-1][1]
ATTN_TM = 1024


LANES = 128


def _to_planes(val, scr, planes):
    tm, n = val.shape
    rows = tm // planes
    for c in range(n // LANES):
        scr[c] = val[:, c * LANES:(c + 1) * LANES]
    return [jnp.concatenate([scr[c, pl.ds(p, rows, stride=planes), :] for c in range(n // LANES)], axis=1)
            for p in range(planes)]


def _from_planes(val, scr, planes):
    tm, n = val.shape
    rows = tm // planes
    for p in range(planes):
        for c in range(n // LANES):
            scr[c, pl.ds(p, rows, stride=planes), :] = val[p * rows:(p + 1) * rows, c * LANES:(c + 1) * LANES]
    return jnp.concatenate([scr[c] for c in range(n // LANES)], axis=1)


QK_ROWS = 32


def _block_rows_ref(ref, r0, nrows):
    if len(ref.shape) == 2:
        return ref.at[r0:r0 + nrows, :]
    per = ref.shape[1]
    return ref.at[r0 // per, r0 % per:r0 % per + nrows, :]


def _rms_rope_store(y_ref, ss_ref, gain_ref, cos_ref, sin_ref, o_ref, plain):
    tm, n = ss_ref.shape
    tw = cos_ref.shape[-1]
    half = HEAD_DIM // 2
    gain = gain_ref[:, 0:tw]
    gain_rot = pltpu.roll(gain, half, 1)
    lane_lo = _pos_in_head((QK_ROWS, n), 1) < half
    for r0 in range(0, tm, QK_ROWS):
        y = jnp.concatenate([y_ref[c, r0:r0 + QK_ROWS, :] for c in range(n // LANES)], axis=1)
        r = jnp.where(plain, 1.0, lax.rsqrt(ss_ref[r0:r0 + QK_ROWS, :] * (1.0 / HEAD_DIM) + NORM_EPS))
        cos = jnp.where(plain, 1.0, _block_rows_ref(cos_ref, r0, QK_ROWS)[...] * gain)
        sin = jnp.where(plain, 0.0, _block_rows_ref(sin_ref, r0, QK_ROWS)[...] * gain_rot)
        cos = jnp.concatenate([cos] * (n // tw), axis=1)
        sin = jnp.concatenate([sin] * (n // tw), axis=1)
        partner = jnp.where(lane_lo, pltpu.roll(y, n - half, 1), pltpu.roll(y, half, 1))
        _block_rows_ref(o_ref, r0, QK_ROWS)[...] = (r * (y * cos + partner * sin)).astype(o_ref.dtype)


def _qkv_kernel(x_ref, mod_ref, g_ref, w_ref, gain_ref, cos_ref, sin_ref, o_ref, h_scr, y_scr, ss_scr,
                *, planes, n_chunks, n_steps):
    t = pl.program_id(0)
    tm = x_ref.shape[0]
    j = jnp.minimum(t, n_steps - 2) % n_chunks
    slot = t % 2

    @pl.when(t == 0)
    def _():
        y_scr[...] = jnp.zeros_like(y_scr)
        ss_scr[...] = jnp.zeros_like(ss_scr)

    @pl.when(jnp.logical_and(j == 0, t < n_steps - 1))
    def _():
        h = _normmod(x_ref[...], g_ref[...], mod_ref[1:2, :], mod_ref[0:1, :])
        if planes == 1:
            h_scr[...] = h.astype(BF16)
        else:
            rows = tm // planes
            for p, hp in enumerate(_to_planes(h, y_scr.at[slot], planes)):
                h_scr[p * rows:(p + 1) * rows, :] = hp.astype(BF16)

    prev_kind = (jnp.maximum(t - 1, 0) % n_chunks) % 3
    ones = _head_ones(LANE_GROUP)

    def step(cur, prev):
        _rms_rope_store(y_scr.at[prev], ss_scr.at[prev], gain_ref, cos_ref, sin_ref, o_ref, prev_kind == 2)
        y = _dot(h_scr[...], w_ref[...])
        n = y.shape[1]
        for c in range(n // LANES):
            y_scr[cur, c] = y[:, c * LANES:(c + 1) * LANES]
        sq = (y * y).astype(BF16)
        for c in range(0, n, LANE_GROUP):
            ss_scr[cur, :, c:c + LANE_GROUP] = _dot(sq[:, c:c + LANE_GROUP], ones)

    pl.when(slot == 0)(lambda: step(0, 1))
    pl.when(slot == 1)(lambda: step(1, 0))


def _qkv_proj(x, mod, g, w, chunk0, n_chunks, gains, cos_t, sin_t, planes):
    bsz, s, d = x.shape
    tn = ATTN_HEADS * HEAD_DIM
    n = n_chunks * tn
    tm = ATTN_TM
    tw = cos_t.shape[-1]
    assert tn == d
    n_tiles = s // tm
    total = bsz * n_tiles * n_chunks

    def where(t):
        tile, j = t // n_chunks, t % n_chunks
        return tile // n_tiles, tile % n_tiles, j

    cur = lambda t: where(jnp.minimum(t, total - 1))
    prev = lambda t: where(jnp.maximum(t - 1, 0))
    if planes == 1:
        tok = lambda width: pl.BlockSpec((None, tm, width), lambda t: (prev(t)[0], prev(t)[1], 0))
        out_spec = pl.BlockSpec((None, tm, tn), lambda t: prev(t))
        out_shape = jax.ShapeDtypeStruct((bsz, s, n), BF16)
    else:
        rows = tm // planes
        tok = lambda width: pl.BlockSpec((None, planes, rows, width), lambda t: (prev(t)[0], 0, prev(t)[1], 0))
        out_spec = pl.BlockSpec((None, planes, rows, tn), lambda t: (prev(t)[0], 0, prev(t)[1], prev(t)[2]))
        out_shape = jax.ShapeDtypeStruct((bsz, planes, s // planes, n), BF16)
    return pl.pallas_call(
        functools.partial(_qkv_kernel, planes=planes, n_chunks=n_chunks, n_steps=total + 1),
        grid=(total + 1,),
        in_specs=[
            pl.BlockSpec((None, tm, d), lambda t: (cur(t)[0], cur(t)[1], 0)),
            pl.BlockSpec((None, 6, d), lambda t: (cur(t)[0], 0, 0)),
            pl.BlockSpec((1, d), lambda t: (0, 0)),
            pl.BlockSpec((d, tn), lambda t: (0, chunk0 + cur(t)[2])),
            pl.BlockSpec((None, 1, tn), lambda t: (prev(t)[2] % 3, 0, 0)),
            tok(tw), tok(tw),
        ],
        out_specs=out_spec,
        out_shape=out_shape,
        scratch_shapes=[pltpu.VMEM((tm, d), BF16), pltpu.VMEM((2, tn // LANES, tm, LANES), F32),
                        pltpu.VMEM((2, tm, tn), F32)],
        compiler_params=_cparams(("arbitrary",)),
        name=f"attn_qkv_p{planes}",
    )(x, mod, g.reshape(1, d), w, gains, cos_t, sin_t)


ATTN_STEP_BLOCKS = 4


def _attn_kernel(q_ref, k_ref, v_ref, kp_ref, vp_ref, o_ref, lse_ref, *, interleave):
    n = pl.program_id(2)
    blk = ATTN_BLOCK
    he = ATTN_HEADS * HEAD_DIM
    seg = blk // interleave

    def pos(axis):
        i = lax.broadcasted_iota(jnp.int32, (blk, blk), axis)
        if interleave == 1:
            return i
        return interleave * jnp.bitwise_and(i, seg - 1) + jnp.right_shift(i, seg.bit_length() - 1)

    def block_of(ref, b):
        if len(ref.shape) == 2:
            return ref[b * blk:(b + 1) * blk, :]
        return ref[:, b * seg:(b + 1) * seg, :].reshape(blk, he)

    qpos, kpos = pos(0), pos(1)
    cur_ok = kpos <= qpos
    prev_ok = kpos >= qpos
    first_ok = kpos >= qpos + jnp.where(n > 0, 0, blk)
    pair = 2 * HEAD_DIM
    lane = lax.broadcasted_iota(jnp.int32, (blk, pair), 1)
    first = lane < HEAD_DIM
    zero = jnp.zeros((), BF16)
    scale = jnp.asarray(HEAD_DIM ** -0.5, BF16)
    sl = lambda hp: slice(hp * pair, (hp + 1) * pair)
    n_blocks = ATTN_STEP_BLOCKS

    q = [block_of(q_ref, b) * scale for b in range(n_blocks)]
    kc = [block_of(k_ref, b) for b in range(n_blocks)]
    vc = [block_of(v_ref, b) for b in range(n_blocks)]
    kp = [kp_ref[...].reshape(blk, he)] + kc[:-1]
    vp = [vp_ref[...].reshape(blk, he)] + vc[:-1]
    masks = [jnp.concatenate([cur_ok, first_ok if b == 0 else prev_ok], axis=1) for b in range(n_blocks)]
    ones = jnp.ones((2 * blk, pair), BF16)
    halves = range(ATTN_HEADS // 2)
    k2 = [[jnp.concatenate([kc[b][:, sl(hp)], kp[b][:, sl(hp)]], axis=0) for hp in halves] for b in range(n_blocks)]
    v2 = [[jnp.concatenate([jnp.concatenate([vc[b][:, sl(hp)], vp[b][:, sl(hp)]], axis=0), ones], axis=1)
           for hp in halves] for b in range(n_blocks)]

    units = [(b, hp, sub) for b in range(n_blocks) for hp in halves for sub in range(2)]
    qh = [jnp.where(first if sub == 0 else jnp.logical_not(first), q[b][:, sl(hp)], zero) for b, hp, sub in units]
    sc = [jnp.where(masks[b], _dot_t(x, k2[b][hp]), NEG_INF) for x, (b, hp, _) in zip(qh, units)]
    m = [jnp.max(a, axis=-1, keepdims=True) for a in sc]
    p = [jnp.exp((a - mx).astype(BF16)) for a, mx in zip(sc, m)]
    acc = [_dot(a, v2[b][hp]) for a, (b, hp, _) in zip(p, units)]
    l = [a[:, pair:] for a in acc]
    o = [a[:, :pair] / dd for a, dd in zip(acc, l)]
    assert lse_ref.shape[-1] == pair
    lse_lane = lax.broadcasted_iota(jnp.int32, (blk, pair), 1)
    per_block = ATTN_HEADS
    for b in range(n_blocks):
        ob = o[b * per_block:(b + 1) * per_block]
        out = jnp.concatenate([jnp.where(first, ob[2 * hp], ob[2 * hp + 1]) for hp in range(ATTN_HEADS // 2)], axis=1)
        lse_tile = jnp.zeros((blk, lse_ref.shape[-1]), F32)
        for idx in range(per_block):
            u = b * per_block + idx
            lse_tile = jnp.where(lse_lane == idx, m[u] + jnp.log(l[u]), lse_tile)
        if len(o_ref.shape) == 2:
            o_ref[b * blk:(b + 1) * blk, :] = out.astype(o_ref.dtype)
            lse_ref[b * blk:(b + 1) * blk, :] = lse_tile
        else:
            o_ref[:, b * seg:(b + 1) * seg, :] = out.astype(o_ref.dtype).reshape(interleave, seg, he)
            lse_ref[:, b * seg:(b + 1) * seg, :] = lse_tile.reshape(interleave, seg, lse_ref.shape[-1])


LSE_LANES = 128


def _dilated_attention(qkv, group):
    he = ATTN_HEADS * HEAD_DIM
    blk = ATTN_BLOCK
    nb = ATTN_STEP_BLOCKS
    dilation = DILATED_GROUPS[group][1]
    if group == 0:
        bsz, s, _ = qkv.shape
        grid = (bsz, 1, s // (nb * blk))
        block = lambda rows, width: (None, rows, width)
        imap = lambda col: (lambda b, r, i: (b, i, col))
        imap_prev = lambda col: (lambda b, r, i: (b, jnp.maximum(nb * i - 1, 0), col))
        out_dims = (bsz, s)
        base, interleave, view, seg = 0, 1, qkv, blk
    else:
        bsz, planes, l2, n = qkv.shape
        interleave = planes // dilation
        seg = blk // interleave
        view = qkv.reshape(bsz, interleave, dilation, l2, n)
        grid = (bsz, dilation, l2 // (nb * seg))
        block = lambda rows, width: (None, interleave, None, rows, width)
        imap = lambda col: (lambda b, r, i: (b, 0, r, i, col))
        imap_prev = lambda col: (lambda b, r, i: (b, 0, r, jnp.maximum(nb * i - 1, 0), col))
        out_dims = (bsz, interleave, dilation, l2)
        base = 3 * (group - 1)
    cur = lambda which: pl.BlockSpec(block(nb * seg, he), imap(base + which))
    prev = lambda which: pl.BlockSpec(block(seg, he), imap_prev(base + which))
    o, lse = pl.pallas_call(
        functools.partial(_attn_kernel, interleave=interleave),
        grid=grid,
        in_specs=[cur(0), cur(1), cur(2), prev(1), prev(2)],
        out_specs=[pl.BlockSpec(block(nb * seg, he), imap(0)), pl.BlockSpec(block(nb * seg, LSE_LANES), imap(0))],
        out_shape=[jax.ShapeDtypeStruct(out_dims + (he,), BF16), jax.ShapeDtypeStruct(out_dims + (LSE_LANES,), F32)],
        compiler_params=_cparams(("parallel", "parallel", "parallel")),
        name=f"dilated_attn_g{group}",
    )(view, view, view, view, view)
    if group == 0:
        return o, lse
    return o.reshape(bsz, planes, l2, he), lse.reshape(bsz, planes, l2, LSE_LANES)


def _merge_kernel(o0_ref, o1_ref, o2_ref, l0_ref, l1_ref, l2_ref, e_ref, w_ref, x_ref, mod_ref, out_ref,
                  tok_scr, lse_scr):
    planes, rows, he = o1_ref.shape
    tm = planes * rows
    l0 = jnp.concatenate(_to_planes(l0_ref[...], lse_scr, planes), axis=0)
    l1 = l1_ref[...].reshape(tm, LSE_LANES)
    l2 = l2_ref[...].reshape(tm, LSE_LANES)
    m = jnp.maximum(jnp.maximum(l0, l1), l2)
    e0, e1, e2 = jnp.exp(l0 - m), jnp.exp(l1 - m), jnp.exp(l2 - m)
    inv = 1.0 / (e0 + e1 + e2)
    expand = e_ref[...]
    widen = lambda wgt: _dot(wgt.astype(BF16), expand)
    part = (widen(e1 * inv) * o1_ref[...].reshape(tm, he).astype(F32)
            + widen(e2 * inv) * o2_ref[...].reshape(tm, he).astype(F32))
    w0 = _from_planes(e0 * inv, lse_scr, planes)
    o = widen(w0) * o0_ref[...].astype(F32) + _from_planes(part, tok_scr, planes)
    y = _dot(o.astype(BF16), w_ref[...])
    out_ref[...] = x_ref[...] + mod_ref[2:3, :] * y


def _attn_merge_proj(outs, lses, w_o, x, mod):
    bsz, s, d = x.shape
    he = outs[0].shape[-1]
    tm = ATTN_TM
    rows = tm // PLANES
    head_of_lane = jnp.arange(he, dtype=jnp.int32) // HEAD_DIM
    expand = (jnp.arange(LSE_LANES, dtype=jnp.int32)[:, None] == head_of_lane[None, :]).astype(BF16)
    tok = lambda width: pl.BlockSpec((None, tm, width), lambda b, i: (b, i, 0))
    pm = lambda width: pl.BlockSpec((None, PLANES, rows, width), lambda b, i: (b, 0, i, 0))
    return pl.pallas_call(
        _merge_kernel,
        grid=(bsz, s // tm),
        in_specs=[tok(he), pm(he), pm(he), tok(LSE_LANES), pm(LSE_LANES), pm(LSE_LANES),
                  pl.BlockSpec((LSE_LANES, he), lambda b, i: (0, 0)),
                  pl.BlockSpec((he, d), lambda b, i: (0, 0)),
                  tok(d),
                  pl.BlockSpec((None, 6, d), lambda b, i: (b, 0, 0))],
        out_specs=tok(d),
        out_shape=jax.ShapeDtypeStruct(x.shape, F32),
        scratch_shapes=[pltpu.VMEM((he // LANES, tm, LANES), F32), pltpu.VMEM((LSE_LANES // LANES, tm, LANES), F32)],
        compiler_params=_cparams(("parallel", "parallel")),
        name="attn_merge_proj",
    )(*outs, *lses, expand, w_o, x, mod)


def _attention_layer(x, mod, g, w_qkv, q_gain, k_gain, w_o, cos_t, sin_t, cos_p, sin_p):
    he = ATTN_HEADS * HEAD_DIM
    gains = jnp.stack([jnp.tile(q_gain, ATTN_HEADS), jnp.tile(k_gain, ATTN_HEADS),
                       jnp.ones((he,), F32)]).reshape(3, 1, he)
    qkv0 = _qkv_proj(x, mod, g, w_qkv, 0, 3, gains, cos_t, sin_t, 1)
    qkv12 = _qkv_proj(x, mod, g, w_qkv, 3, 6, gains, cos_p, sin_p, PLANES)
    outs, lses = zip(_dilated_attention(qkv0, 0), _dilated_attention(qkv12, 1), _dilated_attention(qkv12, 2))
    return _attn_merge_proj(outs, lses, w_o, x, mod)


def _pw1_glu_kernel(x_ref, mod_ref, g_ref, wa_ref, wb_ref, ba_ref, bb_ref, o_ref, h_scr):
    j = pl.program_id(2)

    @pl.when(j == 0)
    def _():
        h = _normmod(x_ref[...], g_ref[...], mod_ref[1:2, :], mod_ref[0:1, :])
        h_scr[...] = h.astype(BF16)

    h = h_scr[...]
    a = _dot(h, wa_ref[...]) + ba_ref[...]
    b = _dot(h, wb_ref[...]) + bb_ref[...]
    o_ref[...] = a * _sigmoid(b)


def _pw1_glu(x, mod, g, w_pw1, b_pw1):
    bsz, s, d = x.shape
    ch = w_pw1.shape[1] // 2
    tm, tn = 1024, 512
    nj = ch // tn
    b2 = b_pw1.reshape(1, 2 * ch)
    return pl.pallas_call(
        _pw1_glu_kernel,
        grid=(bsz, s // tm, nj),
        in_specs=[
            pl.BlockSpec((None, tm, d), lambda b, i, j: (b, i, 0)),
            pl.BlockSpec((None, 6, d), lambda b, i, j: (b, 0, 0)),
            pl.BlockSpec((1, d), lambda b, i, j: (0, 0)),
            pl.BlockSpec((d, tn), lambda b, i, j: (0, j)),
            pl.BlockSpec((d, tn), lambda b, i, j: (0, j + nj)),
            pl.BlockSpec((1, tn), lambda b, i, j: (0, j)),
            pl.BlockSpec((1, tn), lambda b, i, j: (0, j + nj)),
        ],
        out_specs=pl.BlockSpec((None, tm, tn), lambda b, i, j: (b, i, j)),
        out_shape=jax.ShapeDtypeStruct((bsz, s, ch), F32),
        scratch_shapes=[pltpu.VMEM((tm, d), BF16)],
        compiler_params=_cparams(("parallel", "parallel", "arbitrary")),
        name="conv_pw1_glu",
    )(x, mod, g.reshape(1, d), w_pw1, w_pw1, b2, b2)


CONV_HALO = 32
CONV_ROWS = 16
SUBLANES = 8


def _dwconv_kernel(u_ref, halo_ref, wdw_ref, bdw_ref, lng_ref, lnb_ref, w2_ref, b2_ref, x_ref, mod_ref,
                   o_ref, ubuf, cbuf):
    i = pl.program_id(1)
    tm = u_ref.shape[0]
    span = tm + CONV_HALO - SUBLANES
    halo = halo_ref[...]
    ubuf[0, 0:CONV_HALO, :] = jnp.where(i > 0, halo, jnp.zeros_like(halo))
    ubuf[0, CONV_HALO:, :] = u_ref[...]
    for sft in range(1, SUBLANES):
        ubuf[sft, 0:span, :] = ubuf[0, sft:sft + span, :]
    first = CONV_HALO - (CONV_WIDTH - 1)
    for c in range(tm // CONV_ROWS):
        accs = [jnp.zeros((SUBLANES, u_ref.shape[1]), F32) for _ in range(CONV_ROWS // SUBLANES)]
        for k in range(CONV_WIDTH):
            off = c * CONV_ROWS + first + k
            sft = off % SUBLANES
            w8 = wdw_ref[k * SUBLANES:(k + 1) * SUBLANES, :]
            for a in range(len(accs)):
                lo = off - sft + a * SUBLANES
                accs[a] = accs[a] + w8 * ubuf[sft, lo:lo + SUBLANES, :]
        for a, acc in enumerate(accs):
            cbuf[c * CONV_ROWS + a * SUBLANES:c * CONV_ROWS + (a + 1) * SUBLANES, :] = acc
    u = cbuf[...] + bdw_ref[...]
    mu = jnp.mean(u, axis=-1, keepdims=True)
    uc = u - mu
    var = jnp.mean(uc * uc, axis=-1, keepdims=True)
    v = uc * lax.rsqrt(var + NORM_EPS) * lng_ref[...] + lnb_ref[...]
    z = (v * _sigmoid(v)).astype(BF16)
    y = _dot(z, w2_ref[...]) + b2_ref[...]
    o_ref[...] = x_ref[...] + mod_ref[2:3, :] * y


def _dwconv_ln_pw2(u, w_dw, b_dw, ln_g, ln_b, w_pw2, b_pw2, x, mod):
    bsz, s, d = x.shape
    ch = u.shape[-1]
    tm = 512
    ratio = tm // CONV_HALO
    wrep = jnp.repeat(w_dw, SUBLANES, axis=0)
    row = lambda a: a.reshape(1, -1)
    return pl.pallas_call(
        _dwconv_kernel,
        grid=(bsz, s // tm),
        in_specs=[
            pl.BlockSpec((None, tm, ch), lambda b, i: (b, i, 0)),
            pl.BlockSpec((None, CONV_HALO, ch), lambda b, i: (b, jnp.maximum(i * ratio - 1, 0), 0)),
            pl.BlockSpec(wrep.shape, lambda b, i: (0, 0)),
            pl.BlockSpec((1, ch), lambda b, i: (0, 0)),
            pl.BlockSpec((1, ch), lambda b, i: (0, 0)),
            pl.BlockSpec((1, ch), lambda b, i: (0, 0)),
            pl.BlockSpec((ch, d), lambda b, i: (0, 0)),
            pl.BlockSpec((1, d), lambda b, i: (0, 0)),
            pl.BlockSpec((None, tm, d), lambda b, i: (b, i, 0)),
            pl.BlockSpec((None, 6, d), lambda b, i: (b, 0, 0)),
        ],
        out_specs=pl.BlockSpec((None, tm, d), lambda b, i: (b, i, 0)),
        out_shape=jax.ShapeDtypeStruct(x.shape, F32),
        scratch_shapes=[pltpu.VMEM((SUBLANES, tm + CONV_HALO, ch), F32), pltpu.VMEM((tm, ch), F32)],
        compiler_params=_cparams(("parallel", "parallel")),
        name="conv_dw_ln_pw2",
    )(u, u, wrep, row(b_dw), row(ln_g), row(ln_b), w_pw2, row(b_pw2), x, mod)


def _conv_layer(x, mod, g, w_pw1, b_pw1, w_dw, b_dw, ln_g, ln_b, w_pw2, b_pw2):
    u = _pw1_glu(x, mod, g, w_pw1, b_pw1)
    return _dwconv_ln_pw2(u, w_dw, b_dw, ln_g, ln_b, w_pw2, b_pw2, x, mod)


def _rwkv_proj_kernel(x_ref, halo_ref, mod_ref, g_ref, mu_ref, wr_ref, wk_ref, wv_ref,
                      wa_ref, wb_ref, aa_ref, ab_ref, ga_ref, gb_ref, vec_ref,
                      r_ref, ld_ref, k_ref, v_ref, kk_ref, a_ref, gate_ref):
    i = pl.program_id(1)
    g = g_ref[...]
    scale, shift = mod_ref[1:2, :], mod_ref[0:1, :]
    h = _normmod(x_ref[...], g, scale, shift)
    h_halo = _normmod(halo_ref[...], g, scale, shift)
    last = jnp.where(i > 0, h_halo[7:8, :], jnp.zeros_like(h_halo[7:8, :]))
    rows = lax.broadcasted_iota(jnp.int32, h.shape, 0)
    h_prev = jnp.where(rows == 0, last, pltpu.roll(h, 1, 0))
    xx = h_prev - h

    def mix(m):
        return (h + xx * mu_ref[m:m + 1, :]).astype(BF16)

    w0, a0, k_k, k_a = vec_ref[0:1, :], vec_ref[1:2, :], vec_ref[2:3, :], vec_ref[3:4, :]
    r = _dot(mix(0), wr_ref[...])
    k = _dot(mix(2), wk_ref[...])
    v = _dot(mix(3), wv_ref[...])
    zw = w0 + _dot(jnp.tanh(_dot(mix(1), wa_ref[...])).astype(BF16), wb_ref[...])
    w = -(jnp.maximum(-zw, 0.0) + jnp.log1p(jnp.exp(-jnp.abs(zw)))) - 0.5
    a = _sigmoid(a0 + _dot(_dot(mix(4), aa_ref[...]).astype(BF16), ab_ref[...]))
    gate = _dot(_sigmoid(_dot(mix(5), ga_ref[...])).astype(BF16), gb_ref[...])
    kk = k * k_k
    ones = _head_ones(LANE_GROUP)
    sq = (kk * kk).astype(BF16)
    d = kk.shape[1]
    ss = jnp.concatenate([_dot(sq[:, c:c + LANE_GROUP], ones) for c in range(0, d, LANE_GROUP)], axis=1)
    kk = kk / jnp.maximum(jnp.sqrt(ss), 1e-12)
    r_ref[...] = r.astype(r_ref.dtype)
    ld_ref[...] = -jnp.exp(w)
    k_ref[...] = (k * (1.0 + (a - 1.0) * k_a)).astype(k_ref.dtype)
    v_ref[...] = v.astype(v_ref.dtype)
    kk_ref[...] = kk.astype(kk_ref.dtype)
    a_ref[...] = a.astype(a_ref.dtype)
    gate_ref[...] = gate.astype(gate_ref.dtype)


def _rwkv_proj(x, mod, g, mu, w_r, w_k, w_v, wa, wb, aa, ab, ga, gb, vecs):
    bsz, s, d = x.shape
    tm = 256
    ratio = tm // 8
    full = lambda a: pl.BlockSpec(a.shape, lambda b, i: (0,) * a.ndim)
    tok = pl.BlockSpec((None, tm, d), lambda b, i: (b, i, 0))
    mu8 = jnp.zeros((8, d), F32).at[:6].set(mu)
    weights = [w_r, w_k, w_v, wa, wb, aa, ab, ga, gb, vecs]
    return pl.pallas_call(
        _rwkv_proj_kernel,
        grid=(bsz, s // tm),
        in_specs=[tok,
                  pl.BlockSpec((None, 8, d), lambda b, i: (b, jnp.maximum(i * ratio - 1, 0), 0)),
                  pl.BlockSpec((None, 6, d), lambda b, i: (b, 0, 0)),
                  pl.BlockSpec((1, d), lambda b, i: (0, 0)),
                  full(mu8)] + [full(a) for a in weights],
        out_specs=[tok] * 7,
        out_shape=[jax.ShapeDtypeStruct(x.shape, F32 if i == 1 else BF16) for i in range(7)],
        compiler_params=_cparams(("parallel", "parallel")),
        name="rwkv_proj",
    )(x, x, mod, g.reshape(1, d), mu8, *weights)


def _wkv_kernel(r_ref, ld_ref, k_ref, v_ref, kk_ref, a_ref, gate_ref, vec_ref, o_ref, state):
    t = pl.program_id(1)

    @pl.when(t == 0)
    def _():
        state[...] = jnp.zeros_like(state)

    c = CHUNK
    w = LANE_GROUP
    nh = w // HEAD_DIM
    n_streams = r_ref.shape[0]
    n_chunks = r_ref.shape[1] // c
    n_groups = r_ref.shape[2] // w
    lane_sl = [slice(g * w, (g + 1) * w) for g in range(n_groups)]
    assert c == HEAD_DIM

    lane_head = _head_of((c, w), 1)
    head_masks = [lane_head == h for h in range(nh)]
    row_c = lax.broadcasted_iota(jnp.int32, (c, w), 0)
    col_in = _pos_in_head((c, w), 1)
    strict = col_in < row_c
    incl = col_in <= row_c
    eye = (col_in == row_c).astype(F32)
    tri = (lax.broadcasted_iota(jnp.int32, (c, c), 1) <= lax.broadcasted_iota(jnp.int32, (c, c), 0)).astype(BF16)
    diag_blocks = _head_of((w, w), 0) == _head_of((w, w), 1)
    ones = diag_blocks.astype(BF16)
    zero16 = jnp.zeros((), BF16)

    def bd(m):
        m16 = m.astype(BF16)
        return jnp.concatenate([jnp.where(msk, m16, zero16) for msk in head_masks], axis=0)

    def bd_cols(m):
        m16 = m.astype(BF16)
        return jnp.where(diag_blocks, jnp.concatenate([m16] * nh, axis=0), zero16)

    def mm(a, b):
        return _dot(a.astype(BF16), b.astype(BF16))

    def each(fn, *lists):
        return [fn(*args) for args in zip(*lists)]

    def body(ci, carry):
        rows = pl.ds(pl.multiple_of(ci * c, c), c)
        units = [(s, lane_sl[g]) for s in range(n_streams) for g in range(n_groups)]
        ld = [ld_ref[s, rows, ln] for s, ln in units]
        kk = [kk_ref[s, rows, ln].astype(F32) for s, ln in units]
        v = [v_ref[s, rows, ln].astype(F32) for s, ln in units]

        h1 = each(lambda x: x.astype(BF16), ld)
        h2 = each(lambda x, h: (x - h.astype(F32)).astype(BF16), ld, h1)
        c1 = each(lambda h: _dot(tri, h), h1)
        c2 = each(lambda h: _dot(tri, h), h2)
        cum = each(lambda x, y: x + y, c1, c2)
        p_in = each(jnp.exp, cum)
        p_inv = each(lambda x: jnp.exp(-x), cum)
        p_end = each(lambda p: p[c - 1:c, :], p_in)
        at = each(lambda q, x, l: -q * jnp.exp(x - l), kk, cum, ld)
        bt = [q * a_ref[s, rows, ln].astype(F32) * pi for (s, ln), q, pi in zip(units, kk, p_inv)]
        kt = [k_ref[s, rows, ln].astype(F32) * pi for (s, ln), pi in zip(units, p_inv)]
        lhs = [jnp.concatenate([x, r_ref[s, rows, ln].astype(F32) * p], axis=0).astype(BF16)
               for (s, ln), x, p in zip(units, at, p_in)]
        bt_bd = each(bd, bt)
        kt_bd = each(bd, kt)
        g_b = each(_dot_t, lhs, bt_bd)
        g_k = each(_dot_t, lhs, kt_bd)
        a_ab = each(lambda g: jnp.where(strict, g[:c], 0.0), g_b)
        a_ak = each(lambda g: jnp.where(strict, g[:c], 0.0).astype(BF16), g_k)
        b_cat = each(lambda gb, gk: jnp.concatenate(
            [jnp.where(incl, gb[c:], 0.0), jnp.where(incl, gk[c:], 0.0)], axis=1).astype(BF16), g_b, g_k)

        x_inv = each(lambda m: eye + m, a_ab)
        pw = each(lambda m: mm(m, bd_cols(m)), a_ab)
        for _ in range(4):
            both = each(lambda x, p: mm(jnp.concatenate([x, p], axis=0), bd_cols(p)), x_inv, pw)
            x_inv = each(lambda x, b: x + b[:c], x_inv, both)
            pw = each(lambda b: b[c:], both)
        x_inv = each(lambda x, p: (x + mm(x, bd_cols(p))).astype(BF16), x_inv, pw)

        v_bd = each(bd, v)
        akv = each(_dot, a_ak, v_bd)
        bk = each(lambda b, k, p: jnp.concatenate([b * p, k * p], axis=0).astype(BF16), bt, kt, p_end)
        rkr = [(r_ref[s, rows, ln].astype(F32) * k_ref[s, rows, ln].astype(F32) * vec_ref[0:1, ln]).astype(BF16)
               for s, ln in units]
        bonus = each(lambda x, val: _dot(x, ones) * val, rkr, v)

        s_bd = [state[s, g] for s in range(n_streams) for g in range(n_groups)]
        sa_sr = each(lambda l, st: _dot_t(l, st.astype(BF16)), lhs, s_bd)
        u_rhs = each(lambda x, y: bd(x[:c] + y), sa_sr, akv)
        u = each(_dot, x_inv, u_rhs)
        uv_t = each(lambda x, val: jnp.concatenate([x, val], axis=0).T.astype(BF16), u, v)
        upd = each(_dot, uv_t, bk)
        s_new = each(lambda st, p, x: st * p + jnp.where(diag_blocks, x, 0.0), s_bd, p_end, upd)
        i = 0
        for s in range(n_streams):
            for g in range(n_groups):
                state[s, g] = s_new[i]
                i += 1
        y_rhs = each(lambda x, vb: jnp.concatenate([bd(x), vb], axis=0), u, v_bd)
        y2 = each(_dot, b_cat, y_rhs)
        y = each(lambda x, z: x[c:] + z, sa_sr, y2)

        inv_n = 1.0 / HEAD_DIM
        mean = each(lambda x: _dot(x.astype(BF16), ones) * inv_n, y)
        yc = each(lambda x, mu: x - mu, y, mean)
        var = each(lambda x: _dot((x * x).astype(BF16), ones) * inv_n, yc)
        for (s, ln), x, vr, bo in zip(units, yc, var, bonus):
            yn = x * lax.rsqrt(vr + RWKV_GN_EPS) * vec_ref[1:2, ln] + vec_ref[2:3, ln]
            o_ref[s, rows, ln] = ((yn + bo) * gate_ref[s, rows, ln].astype(F32)).astype(o_ref.dtype)
        return carry

    lax.fori_loop(0, n_chunks, body, 0)


WKV_TOKENS = 256


def _wkv_scan(r, ld, k, v, kk, a, gate, vecs):
    bsz, s, d = r.shape
    tb, lw = WKV_TOKENS, d
    tok = pl.BlockSpec((bsz, tb, lw), lambda gi, t: (0, t, gi))
    return pl.pallas_call(
        _wkv_kernel,
        grid=(d // lw, s // tb),
        in_specs=[tok] * 7 + [pl.BlockSpec((8, lw), lambda gi, t: (0, gi))],
        out_specs=tok,
        out_shape=jax.ShapeDtypeStruct((bsz, s, d), BF16),
        scratch_shapes=[pltpu.VMEM((bsz, lw // LANE_GROUP, LANE_GROUP, LANE_GROUP), F32)],
        compiler_params=_cparams(("parallel", "arbitrary")),
        name="wkv7_scan",
    )(r, ld, k, v, kk, a, gate, vecs)


def _rwkv_layer(x, mod, g, mu, w_r, w_k, w_v, w_o, w0, wa, wb, a0, aa, ab, ga, gb, k_k, k_a, r_k, ln_g, ln_b):
    d = x.shape[-1]
    vecs = jnp.zeros((8, d), F32).at[0].set(w0).at[1].set(a0).at[2].set(k_k).at[3].set(k_a)
    r, ld, k, v, kk, a, gate = _rwkv_proj(x, mod, g, mu, w_r, w_k, w_v, wa, wb, aa, ab, ga, gb, vecs)
    svecs = jnp.zeros((8, d), F32).at[0].set(r_k.reshape(d)).at[1].set(ln_g).at[2].set(ln_b)
    z = _wkv_scan(r, ld, k, v, kk, a, gate, svecs)
    return _proj_residual(z, w_o, jnp.zeros((d,), F32), x, mod)


def _rope_tables(positions):
    bsz, s = positions.shape
    half = HEAD_DIM // 2
    inv_freq = jnp.power(ROPE_THETA, -jnp.arange(0, HEAD_DIM, 2, dtype=F32) / HEAD_DIM)
    ang = (positions.astype(F32)[..., None] * inv_freq).reshape(bsz, s * half // LANES, LANES)
    cos, sin = lax.optimization_barrier((jnp.cos(ang), jnp.sin(ang)))
    cos, sin = cos.reshape(bsz, s, half), sin.reshape(bsz, s, half)
    widen = lambda c, sn: (jnp.concatenate([c, c, c, c], -1), jnp.concatenate([-sn, sn, -sn, sn], -1))
    return widen(cos, sin), widen(_plane_major(cos), _plane_major(sin))


def _plane_major(t):
    bsz, s, n = t.shape
    return t.reshape(bsz, s // PLANES, PLANES, n).transpose(0, 2, 1, 3)


def kernel(x, c, positions, ada_w, ada_b, norm_mix_g, norm_ffn_g, ffn_w_gate, ffn_w_up, ffn_w_down, attn_w_qkv, attn_q_gain, attn_k_gain, attn_w_o, conv_w_pw1, conv_b_pw1, conv_w_dw, conv_b_dw, conv_ln_g, conv_ln_b, conv_w_pw2, conv_b_pw2, rwkv_mu, rwkv_w_r, rwkv_w_k, rwkv_w_v, rwkv_w_o, rwkv_w0, rwkv_w_lora_a, rwkv_w_lora_b, rwkv_a0, rwkv_a_lora_a, rwkv_a_lora_b, rwkv_g_lora_a, rwkv_g_lora_b, rwkv_k_k, rwkv_k_a, rwkv_r_k, rwkv_ln_g, rwkv_ln_b):
    depth = ada_w.shape[0]
    bf = lambda a: a.astype(BF16)
    (cos_t, sin_t), (cos_p, sin_p) = _rope_tables(positions)
    mods = _ada_mod(c, ada_w, ada_b)
    for i in range(depth):
        mod = mods[i]
        kind, j = i % 3, i // 3
        if kind == 0:
            x = _attention_layer(x, mod, norm_mix_g[i], bf(attn_w_qkv[j]), attn_q_gain[j], attn_k_gain[j],
                                 bf(attn_w_o[j]), cos_t, sin_t, cos_p, sin_p)
        elif kind == 1:
            x = _conv_layer(x, mod, norm_mix_g[i], bf(conv_w_pw1[j]), conv_b_pw1[j], conv_w_dw[j], conv_b_dw[j],
                            conv_ln_g[j], conv_ln_b[j], bf(conv_w_pw2[j]), conv_b_pw2[j])
        else:
            x = _rwkv_layer(x, mod, norm_mix_g[i], rwkv_mu[j], bf(rwkv_w_r[j]), bf(rwkv_w_k[j]), bf(rwkv_w_v[j]),
                            bf(rwkv_w_o[j]), rwkv_w0[j], bf(rwkv_w_lora_a[j]), bf(rwkv_w_lora_b[j]), rwkv_a0[j],
                            bf(rwkv_a_lora_a[j]), bf(rwkv_a_lora_b[j]), bf(rwkv_g_lora_a[j]), bf(rwkv_g_lora_b[j]),
                            rwkv_k_k[j], rwkv_k_a[j], rwkv_r_k[j], rwkv_ln_g[j], rwkv_ln_b[j])
        x = _ffn(x, mod, norm_ffn_g[i], bf(ffn_w_gate[i]), bf(ffn_w_up[i]), bf(ffn_w_down[i]))
    return x
```

```python
import functools

import jax
import jax.numpy as jnp
from jax import lax
from jax.experimental import pallas as pl
from jax.experimental.pallas import tpu as pltpu

F32 = jnp.float32
BF16 = jnp.bfloat16

NORM_EPS = 1e-6
HEAD_DIM = 64
ATTN_HEADS = 16
DILATED_GROUPS = ((128, 1), (512, 4), (2048, 16))
ATTN_BLOCK = 128
ROPE_THETA = 10000.0
NEG_INF = -1e30
CONV_WIDTH = 31
RWKV_GN_EPS = 1e-5 * HEAD_DIM
CHUNK = 64
LANE_GROUP = 256
VMEM_LIMIT = 56 * 1024 * 1024


def _cparams(sem):
    return pltpu.CompilerParams(dimension_semantics=sem, vmem_limit_bytes=VMEM_LIMIT)


def _dot(a, b):
    return jnp.dot(a, b, preferred_element_type=F32)


def _dot_t(a, b):
    return lax.dot_general(a, b, (((1,), (1,)), ((), ())), preferred_element_type=F32)


def _sigmoid(x):
    return 1.0 / (1.0 + jnp.exp(-x))


def _normmod(x, g, scale, shift):
    ms = jnp.mean(x * x, axis=-1, keepdims=True)
    return x * lax.rsqrt(ms + NORM_EPS) * g * (1.0 + scale) + shift


HEAD_SHIFT = HEAD_DIM.bit_length() - 1
assert 1 << HEAD_SHIFT == HEAD_DIM


def _head_of(shape, axis):
    return jnp.right_shift(lax.broadcasted_iota(jnp.int32, shape, axis), HEAD_SHIFT)


def _pos_in_head(shape, axis):
    return jnp.bitwise_and(lax.broadcasted_iota(jnp.int32, shape, axis), HEAD_DIM - 1)


def _head_ones(n):
    return (_head_of((n, n), 0) == _head_of((n, n), 1)).astype(BF16)


def _ada_kernel(c_ref, w_ref, b_ref, o_ref):
    c = c_ref[...]
    c_act = (c * _sigmoid(c)).astype(BF16)
    o_ref[...] = _dot(c_act, w_ref[...].astype(BF16)) + b_ref[...]


def _ada_mod(c, ada_w, ada_b):
    depth, d, n = ada_w.shape
    bsz = c.shape[0]
    rows = 8
    c_pad = jnp.zeros((rows, d), F32).at[:bsz].set(c)
    tn = 1536
    out = pl.pallas_call(
        _ada_kernel,
        grid=(depth, n // tn),
        in_specs=[
            pl.BlockSpec((rows, d), lambda i, j: (0, 0)),
            pl.BlockSpec((None, d, tn), lambda i, j: (i, 0, j)),
            pl.BlockSpec((None, 1, tn), lambda i, j: (i, 0, j)),
        ],
        out_specs=pl.BlockSpec((None, rows, tn), lambda i, j: (i, 0, j)),
        out_shape=jax.ShapeDtypeStruct((depth, rows, n), F32),
        compiler_params=_cparams(("parallel", "parallel")),
        name="ada_mod",
    )(c_pad, ada_w, ada_b.reshape(depth, 1, n))
    return out[:, :bsz].reshape(depth, bsz, 6, d)


FFN_SUB = 256


def _ffn_kernel(x_ref, mod_ref, g_ref, wg_ref, wu_ref, wd_ref, o_ref, h_scr, acc_scr):
    j = pl.program_id(2)

    @pl.when(j == 0)
    def _():
        h = _normmod(x_ref[...], g_ref[...], mod_ref[4:5, :], mod_ref[3:4, :])
        h_scr[...] = h.astype(BF16)
        acc_scr[...] = jnp.zeros_like(acc_scr)

    h = h_scr[...]
    tf = wg_ref.shape[1]
    cols = [(c0, min(c0 + FFN_SUB, tf)) for c0 in range(0, tf, FFN_SUB)]
    gate_up = lambda c: (_dot(h, wg_ref[:, c[0]:c[1]]), _dot(h, wu_ref[:, c[0]:c[1]]))
    nxt = gate_up(cols[0])
    for idx, c in enumerate(cols):
        a, u = nxt
        if idx + 1 < len(cols):
            nxt = gate_up(cols[idx + 1])
        z = (a * _sigmoid(a) * u).astype(BF16)
        acc_scr[...] += _dot(z, wd_ref[c[0]:c[1], :])

    @pl.when(j == pl.num_programs(2) - 1)
    def _():
        o_ref[...] = x_ref[...] + mod_ref[5:6, :] * acc_scr[...]


def _ffn(x, mod, g, wg, wu, wd):
    bsz, s, d = x.shape
    f = wg.shape[1]
    tm, tf = 1024, f // 2
    return pl.pallas_call(
        _ffn_kernel,
        grid=(bsz, s // tm, f // tf),
        in_specs=[
            pl.BlockSpec((None, tm, d), lambda b, i, j: (b, i, 0)),
            pl.BlockSpec((None, 6, d), lambda b, i, j: (b, 0, 0)),
            pl.BlockSpec((1, d), lambda b, i, j: (0, 0)),
            pl.BlockSpec((d, tf), lambda b, i, j: (0, j)),
            pl.BlockSpec((d, tf), lambda b, i, j: (0, j)),
            pl.BlockSpec((tf, d), lambda b, i, j: (j, 0)),
        ],
        out_specs=pl.BlockSpec((None, tm, d), lambda b, i, j: (b, i, 0)),
        out_shape=jax.ShapeDtypeStruct(x.shape, F32),
        scratch_shapes=[pltpu.VMEM((tm, d), BF16), pltpu.VMEM((tm, d), F32)],
        compiler_params=_cparams(("parallel", "parallel", "arbitrary")),
        name="ffn",
    )(x, mod, g.reshape(1, d), wg, wu, wd)


def _proj_res_kernel(z_ref, w_ref, b_ref, x_ref, mod_ref, o_ref):
    y = _dot(z_ref[...], w_ref[...]) + b_ref[...]
    o_ref[...] = x_ref[...] + mod_ref[2:3, :] * y


def _proj_residual(z, w, bias, x, mod):
    bsz, s, d = x.shape
    k = z.shape[-1]
    tm = 1024
    return pl.pallas_call(
        _proj_res_kernel,
        grid=(bsz, s // tm),
        in_specs=[
            pl.BlockSpec((None, tm, k), lambda b, i: (b, i, 0)),
            pl.BlockSpec((k, d), lambda b, i: (0, 0)),
            pl.BlockSpec((1, d), lambda b, i: (0, 0)),
            pl.BlockSpec((None, tm, d), lambda b, i: (b, i, 0)),
            pl.BlockSpec((None, 6, d), lambda b, i: (b, 0, 0)),
        ],
        out_specs=pl.BlockSpec((None, tm, d), lambda b, i: (b, i, 0)),
        out_shape=jax.ShapeDtypeStruct(x.shape, F32),
        compiler_params=_cparams(("parallel", "parallel")),
        name="proj_residual",
    )(z, w, bias.reshape(1, d), x, mod)


PLANES = DILATED_GROUPS[-1][1]
ATTN_TM = 1024


LANES = 128


def _to_planes(val, scr, planes):
    tm, n = val.shape
    rows = tm // planes
    for c in range(n // LANES):
        scr[c] = val[:, c * LANES:(c + 1) * LANES]
    return [jnp.concatenate([scr[c, pl.ds(p, rows, stride=planes), :] for c in range(n // LANES)], axis=1)
            for p in range(planes)]


def _from_planes(val, scr, planes):
    tm, n = val.shape
    rows = tm // planes
    for p in range(planes):
        for c in range(n // LANES):
            scr[c, pl.ds(p, rows, stride=planes), :] = val[p * rows:(p + 1) * rows, c * LANES:(c + 1) * LANES]
    return jnp.concatenate([scr[c] for c in range(n // LANES)], axis=1)


QK_ROWS = 32


def _block_rows_ref(ref, r0, nrows):
    if len(ref.shape) == 2:
        return ref.at[r0:r0 + nrows, :]
    per = ref.shape[1]
    return ref.at[r0 // per, r0 % per:r0 % per + nrows, :]


def _rms_rope_store(y_ref, ss_ref, gain_ref, cos_ref, sin_ref, o_ref, plain):
    tm, n = ss_ref.shape
    tw = cos_ref.shape[-1]
    half = HEAD_DIM // 2
    gain = gain_ref[:, 0:tw]
    gain_rot = pltpu.roll(gain, half, 1)
    lane_lo = _pos_in_head((QK_ROWS, n), 1) < half
    for r0 in range(0, tm, QK_ROWS):
        y = jnp.concatenate([y_ref[c, r0:r0 + QK_ROWS, :] for c in range(n // LANES)], axis=1)
        r = jnp.where(plain, 1.0, lax.rsqrt(ss_ref[r0:r0 + QK_ROWS, :] * (1.0 / HEAD_DIM) + NORM_EPS))
        cos = jnp.where(plain, 1.0, _block_rows_ref(cos_ref, r0, QK_ROWS)[...] * gain)
        sin = jnp.where(plain, 0.0, _block_rows_ref(sin_ref, r0, QK_ROWS)[...] * gain_rot)
        cos = jnp.concatenate([cos] * (n // tw), axis=1)
        sin = jnp.concatenate([sin] * (n // tw), axis=1)
        partner = jnp.where(lane_lo, pltpu.roll(y, n - half, 1), pltpu.roll(y, half, 1))
        _block_rows_ref(o_ref, r0, QK_ROWS)[...] = (r * (y * cos + partner * sin)).astype(o_ref.dtype)


def _qkv_kernel(x_ref, mod_ref, g_ref, w_ref, gain_ref, cos_ref, sin_ref, o_ref, h_scr, y_scr, ss_scr,
                *, planes, n_chunks, n_steps):
    t = pl.program_id(0)
    tm = x_ref.shape[0]
    j = jnp.minimum(t, n_steps - 2) % n_chunks
    slot = t % 2

    @pl.when(t == 0)
    def _():
        y_scr[...] = jnp.zeros_like(y_scr)
        ss_scr[...] = jnp.zeros_like(ss_scr)

    @pl.when(jnp.logical_and(j == 0, t < n_steps - 1))
    def _():
        h = _normmod(x_ref[...], g_ref[...], mod_ref[1:2, :], mod_ref[0:1, :])
        if planes == 1:
            h_scr[...] = h.astype(BF16)
        else:
            rows = tm // planes
            for p, hp in enumerate(_to_planes(h, y_scr.at[slot], planes)):
                h_scr[p * rows:(p + 1) * rows, :] = hp.astype(BF16)

    prev_kind = (jnp.maximum(t - 1, 0) % n_chunks) % 3
    ones = _head_ones(LANE_GROUP)

    def step(cur, prev):
        _rms_rope_store(y_scr.at[prev], ss_scr.at[prev], gain_ref, cos_ref, sin_ref, o_ref, prev_kind == 2)
        y = _dot(h_scr[...], w_ref[...])
        n = y.shape[1]
        for c in range(n // LANES):
            y_scr[cur, c] = y[:, c * LANES:(c + 1) * LANES]
        sq = (y * y).astype(BF16)
        for c in range(0, n, LANE_GROUP):
            ss_scr[cur, :, c:c + LANE_GROUP] = _dot(sq[:, c:c + LANE_GROUP], ones)

    pl.when(slot == 0)(lambda: step(0, 1))
    pl.when(slot == 1)(lambda: step(1, 0))


def _qkv_proj(x, mod, g, w, chunk0, n_chunks, gains, cos_t, sin_t, planes):
    bsz, s, d = x.shape
    tn = ATTN_HEADS * HEAD_DIM
    n = n_chunks * tn
    tm = ATTN_TM
    tw = cos_t.shape[-1]
    assert tn == d
    n_tiles = s // tm
    total = bsz * n_tiles * n_chunks

    def where(t):
        tile, j = t // n_chunks, t % n_chunks
        return tile // n_tiles, tile % n_tiles, j

    cur = lambda t: where(jnp.minimum(t, total - 1))
    prev = lambda t: where(jnp.maximum(t - 1, 0))
    if planes == 1:
        tok = lambda width: pl.BlockSpec((None, tm, width), lambda t: (prev(t)[0], prev(t)[1], 0))
        out_spec = pl.BlockSpec((None, tm, tn), lambda t: prev(t))
        out_shape = jax.ShapeDtypeStruct((bsz, s, n), BF16)
    else:
        rows = tm // planes
        tok = lambda width: pl.BlockSpec((None, planes, rows, width), lambda t: (prev(t)[0], 0, prev(t)[1], 0))
        out_spec = pl.BlockSpec((None, planes, rows, tn), lambda t: (prev(t)[0], 0, prev(t)[1], prev(t)[2]))
        out_shape = jax.ShapeDtypeStruct((bsz, planes, s // planes, n), BF16)
    return pl.pallas_call(
        functools.partial(_qkv_kernel, planes=planes, n_chunks=n_chunks, n_steps=total + 1),
        grid=(total + 1,),
        in_specs=[
            pl.BlockSpec((None, tm, d), lambda t: (cur(t)[0], cur(t)[1], 0)),
            pl.BlockSpec((None, 6, d), lambda t: (cur(t)[0], 0, 0)),
            pl.BlockSpec((1, d), lambda t: (0, 0)),
            pl.BlockSpec((d, tn), lambda t: (0, chunk0 + cur(t)[2])),
            pl.BlockSpec((None, 1, tn), lambda t: (prev(t)[2] % 3, 0, 0)),
            tok(tw), tok(tw),
        ],
        out_specs=out_spec,
        out_shape=out_shape,
        scratch_shapes=[pltpu.VMEM((tm, d), BF16), pltpu.VMEM((2, tn // LANES, tm, LANES), F32),
                        pltpu.VMEM((2, tm, tn), F32)],
        compiler_params=_cparams(("arbitrary",)),
        name=f"attn_qkv_p{planes}",
    )(x, mod, g.reshape(1, d), w, gains, cos_t, sin_t)


ATTN_STEP_BLOCKS = 4


def _attn_kernel(q_ref, k_ref, v_ref, kp_ref, vp_ref, o_ref, lse_ref, *, interleave):
    n = pl.program_id(2)
    blk = ATTN_BLOCK
    he = ATTN_HEADS * HEAD_DIM
    seg = blk // interleave

    def pos(axis):
        i = lax.broadcasted_iota(jnp.int32, (blk, blk), axis)
        if interleave == 1:
            return i
        return interleave * jnp.bitwise_and(i, seg - 1) + jnp.right_shift(i, seg.bit_length() - 1)

    def block_of(ref, b):
        if len(ref.shape) == 2:
            return ref[b * blk:(b + 1) * blk, :]
        return ref[:, b * seg:(b + 1) * seg, :].reshape(blk, he)

    qpos, kpos = pos(0), pos(1)
    cur_ok = kpos <= qpos
    prev_ok = kpos >= qpos
    first_ok = kpos >= qpos + jnp.where(n > 0, 0, blk)
    pair = 2 * HEAD_DIM
    lane = lax.broadcasted_iota(jnp.int32, (blk, pair), 1)
    first = lane < HEAD_DIM
    zero = jnp.zeros((), BF16)
    scale = jnp.asarray(HEAD_DIM ** -0.5, BF16)
    sl = lambda hp: slice(hp * pair, (hp + 1) * pair)
    n_blocks = ATTN_STEP_BLOCKS

    q = [block_of(q_ref, b) * scale for b in range(n_blocks)]
    kc = [block_of(k_ref, b) for b in range(n_blocks)]
    vc = [block_of(v_ref, b) for b in range(n_blocks)]
    kp = [kp_ref[...].reshape(blk, he)] + kc[:-1]
    vp = [vp_ref[...].reshape(blk, he)] + vc[:-1]
    masks = [jnp.concatenate([cur_ok, first_ok if b == 0 else prev_ok], axis=1) for b in range(n_blocks)]
    ones = jnp.ones((2 * blk, pair), BF16)
    halves = range(ATTN_HEADS // 2)
    k2 = [[jnp.concatenate([kc[b][:, sl(hp)], kp[b][:, sl(hp)]], axis=0) for hp in halves] for b in range(n_blocks)]
    v2 = [[jnp.concatenate([jnp.concatenate([vc[b][:, sl(hp)], vp[b][:, sl(hp)]], axis=0), ones], axis=1)
           for hp in halves] for b in range(n_blocks)]

    units = [(b, hp, sub) for b in range(n_blocks) for hp in halves for sub in range(2)]
    qh = [jnp.where(first if sub == 0 else jnp.logical_not(first), q[b][:, sl(hp)], zero) for b, hp, sub in units]
    sc = [jnp.where(masks[b], _dot_t(x, k2[b][hp]), NEG_INF) for x, (b, hp, _) in zip(qh, units)]
    m = [jnp.max(a, axis=-1, keepdims=True) for a in sc]
    p = [jnp.exp((a - mx).astype(BF16)) for a, mx in zip(sc, m)]
    acc = [_dot(a, v2[b][hp]) for a, (b, hp, _) in zip(p, units)]
    l = [a[:, pair:] for a in acc]
    o = [a[:, :pair] / dd for a, dd in zip(acc, l)]
    assert lse_ref.shape[-1] == pair
    lse_lane = lax.broadcasted_iota(jnp.int32, (blk, pair), 1)
    per_block = ATTN_HEADS
    for b in range(n_blocks):
        ob = o[b * per_block:(b + 1) * per_block]
        out = jnp.concatenate([jnp.where(first, ob[2 * hp], ob[2 * hp + 1]) for hp in range(ATTN_HEADS // 2)], axis=1)
        lse_tile = jnp.zeros((blk, lse_ref.shape[-1]), F32)
        for idx in range(per_block):
            u = b * per_block + idx
            lse_tile = jnp.where(lse_lane == idx, m[u] + jnp.log(l[u]), lse_tile)
        if len(o_ref.shape) == 2:
            o_ref[b * blk:(b + 1) * blk, :] = out.astype(o_ref.dtype)
            lse_ref[b * blk:(b + 1) * blk, :] = lse_tile
        else:
            o_ref[:, b * seg:(b + 1) * seg, :] = out.astype(o_ref.dtype).reshape(interleave, seg, he)
            lse_ref[:, b * seg:(b + 1) * seg, :] = lse_tile.reshape(interleave, seg, lse_ref.shape[-1])


LSE_LANES = 128


def _dilated_attention(qkv, group):
    he = ATTN_HEADS * HEAD_DIM
    blk = ATTN_BLOCK
    nb = ATTN_STEP_BLOCKS
    dilation = DILATED_GROUPS[group][1]
    if group == 0:
        bsz, s, _ = qkv.shape
        grid = (bsz, 1, s // (nb * blk))
        block = lambda rows, width: (None, rows, width)
        imap = lambda col: (lambda b, r, i: (b, i, col))
        imap_prev = lambda col: (lambda b, r, i: (b, jnp.maximum(nb * i - 1, 0), col))
        out_dims = (bsz, s)
        base, interleave, view, seg = 0, 1, qkv, blk
    else:
        bsz, planes, l2, n = qkv.shape
        interleave = planes // dilation
        seg = blk // interleave
        view = qkv.reshape(bsz, interleave, dilation, l2, n)
        grid = (bsz, dilation, l2 // (nb * seg))
        block = lambda rows, width: (None, interleave, None, rows, width)
        imap = lambda col: (lambda b, r, i: (b, 0, r, i, col))
        imap_prev = lambda col: (lambda b, r, i: (b, 0, r, jnp.maximum(nb * i - 1, 0), col))
        out_dims = (bsz, interleave, dilation, l2)
        base = 3 * (group - 1)
    cur = lambda which: pl.BlockSpec(block(nb * seg, he), imap(base + which))
    prev = lambda which: pl.BlockSpec(block(seg, he), imap_prev(base + which))
    o, lse = pl.pallas_call(
        functools.partial(_attn_kernel, interleave=interleave),
        grid=grid,
        in_specs=[cur(0), cur(1), cur(2), prev(1), prev(2)],
        out_specs=[pl.BlockSpec(block(nb * seg, he), imap(0)), pl.BlockSpec(block(nb * seg, LSE_LANES), imap(0))],
        out_shape=[jax.ShapeDtypeStruct(out_dims + (he,), BF16), jax.ShapeDtypeStruct(out_dims + (LSE_LANES,), F32)],
        compiler_params=_cparams(("parallel", "parallel", "parallel")),
        name=f"dilated_attn_g{group}",
    )(view, view, view, view, view)
    if group == 0:
        return o, lse
    return o.reshape(bsz, planes, l2, he), lse.reshape(bsz, planes, l2, LSE_LANES)


def _merge_kernel(o0_ref, o1_ref, o2_ref, l0_ref, l1_ref, l2_ref, e_ref, w_ref, x_ref, mod_ref, out_ref,
                  tok_scr, lse_scr):
    planes, rows, he = o1_ref.shape
    tm = planes * rows
    l0 = jnp.concatenate(_to_planes(l0_ref[...], lse_scr, planes), axis=0)
    l1 = l1_ref[...].reshape(tm, LSE_LANES)
    l2 = l2_ref[...].reshape(tm, LSE_LANES)
    m = jnp.maximum(jnp.maximum(l0, l1), l2)
    e0, e1, e2 = jnp.exp(l0 - m), jnp.exp(l1 - m), jnp.exp(l2 - m)
    inv = 1.0 / (e0 + e1 + e2)
    expand = e_ref[...]
    widen = lambda wgt: _dot(wgt.astype(BF16), expand)
    part = (widen(e1 * inv) * o1_ref[...].reshape(tm, he).astype(F32)
            + widen(e2 * inv) * o2_ref[...].reshape(tm, he).astype(F32))
    w0 = _from_planes(e0 * inv, lse_scr, planes)
    o = widen(w0) * o0_ref[...].astype(F32) + _from_planes(part, tok_scr, planes)
    y = _dot(o.astype(BF16), w_ref[...])
    out_ref[...] = x_ref[...] + mod_ref[2:3, :] * y


def _attn_merge_proj(outs, lses, w_o, x, mod):
    bsz, s, d = x.shape
    he = outs[0].shape[-1]
    tm = ATTN_TM
    rows = tm // PLANES
    head_of_lane = jnp.arange(he, dtype=jnp.int32) // HEAD_DIM
    expand = (jnp.arange(LSE_LANES, dtype=jnp.int32)[:, None] == head_of_lane[None, :]).astype(BF16)
    tok = lambda width: pl.BlockSpec((None, tm, width), lambda b, i: (b, i, 0))
    pm = lambda width: pl.BlockSpec((None, PLANES, rows, width), lambda b, i: (b, 0, i, 0))
    return pl.pallas_call(
        _merge_kernel,
        grid=(bsz, s // tm),
        in_specs=[tok(he), pm(he), pm(he), tok(LSE_LANES), pm(LSE_LANES), pm(LSE_LANES),
                  pl.BlockSpec((LSE_LANES, he), lambda b, i: (0, 0)),
                  pl.BlockSpec((he, d), lambda b, i: (0, 0)),
                  tok(d),
                  pl.BlockSpec((None, 6, d), lambda b, i: (b, 0, 0))],
        out_specs=tok(d),
        out_shape=jax.ShapeDtypeStruct(x.shape, F32),
        scratch_shapes=[pltpu.VMEM((he // LANES, tm, LANES), F32), pltpu.VMEM((LSE_LANES // LANES, tm, LANES), F32)],
        compiler_params=_cparams(("parallel", "parallel")),
        name="attn_merge_proj",
    )(*outs, *lses, expand, w_o, x, mod)


def _attention_layer(x, mod, g, w_qkv, q_gain, k_gain, w_o, cos_t, sin_t, cos_p, sin_p):
    he = ATTN_HEADS * HEAD_DIM
    gains = jnp.stack([jnp.tile(q_gain, ATTN_HEADS), jnp.tile(k_gain, ATTN_HEADS),
                       jnp.ones((he,), F32)]).reshape(3, 1, he)
    qkv0 = _qkv_proj(x, mod, g, w_qkv, 0, 3, gains, cos_t, sin_t, 1)
    qkv12 = _qkv_proj(x, mod, g, w_qkv, 3, 6, gains, cos_p, sin_p, PLANES)
    outs, lses = zip(_dilated_attention(qkv0, 0), _dilated_attention(qkv12, 1), _dilated_attention(qkv12, 2))
    return _attn_merge_proj(outs, lses, w_o, x, mod)


def _pw1_glu_kernel(x_ref, mod_ref, g_ref, wa_ref, wb_ref, ba_ref, bb_ref, o_ref, h_scr):
    j = pl.program_id(2)

    @pl.when(j == 0)
    def _():
        h = _normmod(x_ref[...], g_ref[...], mod_ref[1:2, :], mod_ref[0:1, :])
        h_scr[...] = h.astype(BF16)

    h = h_scr[...]
    a = _dot(h, wa_ref[...]) + ba_ref[...]
    b = _dot(h, wb_ref[...]) + bb_ref[...]
    o_ref[...] = a * _sigmoid(b)


def _pw1_glu(x, mod, g, w_pw1, b_pw1):
    bsz, s, d = x.shape
    ch = w_pw1.shape[1] // 2
    tm, tn = 1024, 512
    nj = ch // tn
    b2 = b_pw1.reshape(1, 2 * ch)
    return pl.pallas_call(
        _pw1_glu_kernel,
        grid=(bsz, s // tm, nj),
        in_specs=[
            pl.BlockSpec((None, tm, d), lambda b, i, j: (b, i, 0)),
            pl.BlockSpec((None, 6, d), lambda b, i, j: (b, 0, 0)),
            pl.BlockSpec((1, d), lambda b, i, j: (0, 0)),
            pl.BlockSpec((d, tn), lambda b, i, j: (0, j)),
            pl.BlockSpec((d, tn), lambda b, i, j: (0, j + nj)),
            pl.BlockSpec((1, tn), lambda b, i, j: (0, j)),
            pl.BlockSpec((1, tn), lambda b, i, j: (0, j + nj)),
        ],
        out_specs=pl.BlockSpec((None, tm, tn), lambda b, i, j: (b, i, j)),
        out_shape=jax.ShapeDtypeStruct((bsz, s, ch), F32),
        scratch_shapes=[pltpu.VMEM((tm, d), BF16)],
        compiler_params=_cparams(("parallel", "parallel", "arbitrary")),
        name="conv_pw1_glu",
    )(x, mod, g.reshape(1, d), w_pw1, w_pw1, b2, b2)


CONV_HALO = 32
CONV_ROWS = 16
SUBLANES = 8


def _dwconv_kernel(u_ref, halo_ref, wdw_ref, bdw_ref, lng_ref, lnb_ref, w2_ref, b2_ref, x_ref, mod_ref,
                   o_ref, ubuf, cbuf):
    i = pl.program_id(1)
    tm = u_ref.shape[0]
    span = tm + CONV_HALO - SUBLANES
    halo = halo_ref[...]
    ubuf[0, 0:CONV_HALO, :] = jnp.where(i > 0, halo, jnp.zeros_like(halo))
    ubuf[0, CONV_HALO:, :] = u_ref[...]
    for sft in range(1, SUBLANES):
        ubuf[sft, 0:span, :] = ubuf[0, sft:sft + span, :]
    first = CONV_HALO - (CONV_WIDTH - 1)
    for c in range(tm // CONV_ROWS):
        accs = [jnp.zeros((SUBLANES, u_ref.shape[1]), F32) for _ in range(CONV_ROWS // SUBLANES)]
        for k in range(CONV_WIDTH):
            off = c * CONV_ROWS + first + k
            sft = off % SUBLANES
            w8 = wdw_ref[k * SUBLANES:(k + 1) * SUBLANES, :]
            for a in range(len(accs)):
                lo = off - sft + a * SUBLANES
                accs[a] = accs[a] + w8 * ubuf[sft, lo:lo + SUBLANES, :]
        for a, acc in enumerate(accs):
            cbuf[c * CONV_ROWS + a * SUBLANES:c * CONV_ROWS + (a + 1) * SUBLANES, :] = acc
    u = cbuf[...] + bdw_ref[...]
    mu = jnp.mean(u, axis=-1, keepdims=True)
    uc = u - mu
    var = jnp.mean(uc * uc, axis=-1, keepdims=True)
    v = uc * lax.rsqrt(var + NORM_EPS) * lng_ref[...] + lnb_ref[...]
    z = (v * _sigmoid(v)).astype(BF16)
    y = _dot(z, w2_ref[...]) + b2_ref[...]
    o_ref[...] = x_ref[...] + mod_ref[2:3, :] * y


def _conv_fused_kernel(x_ref, xh_ref, mod_ref, g_ref, w1_ref, b1_ref, wdw_ref, bdw_ref, lng_ref, lnb_ref, w2_ref,
                       b2_ref, o_ref, u_scr, halo_scr, ubuf, cbuf):
    ch = u_scr.shape[1]

    def glu(xv):
        h = _normmod(xv, g_ref[...], mod_ref[1:2, :], mod_ref[0:1, :]).astype(BF16)
        a = _dot(h, w1_ref[:, 0:ch]) + b1_ref[:, 0:ch]
        b = _dot(h, w1_ref[:, ch:2 * ch]) + b1_ref[:, ch:2 * ch]
        return a * _sigmoid(b)

    u_scr[...] = glu(x_ref[...])
    halo_scr[...] = glu(xh_ref[...])
    _dwconv_kernel(u_scr, halo_scr, wdw_ref, bdw_ref, lng_ref, lnb_ref, w2_ref, b2_ref, x_ref, mod_ref,
                   o_ref, ubuf, cbuf)


def _conv_layer(x, mod, g, w_pw1, b_pw1, w_dw, b_dw, ln_g, ln_b, w_pw2, b_pw2):
    bsz, s, d = x.shape
    ch = w_pw2.shape[0]
    tm = 512
    ratio = tm // CONV_HALO
    wrep = jnp.repeat(w_dw, SUBLANES, axis=0)
    row = lambda a: a.reshape(1, -1)
    const = lambda shape: pl.BlockSpec(shape, lambda b, i: (0, 0))
    return pl.pallas_call(
        _conv_fused_kernel,
        grid=(bsz, s // tm),
        in_specs=[
            pl.BlockSpec((None, tm, d), lambda b, i: (b, i, 0)),
            pl.BlockSpec((None, CONV_HALO, d), lambda b, i: (b, jnp.maximum(i * ratio - 1, 0), 0)),
            pl.BlockSpec((None, 6, d), lambda b, i: (b, 0, 0)),
            const((1, d)), const((d, 2 * ch)), const((1, 2 * ch)), const(wrep.shape),
            const((1, ch)), const((1, ch)), const((1, ch)), const((ch, d)), const((1, d)),
        ],
        out_specs=pl.BlockSpec((None, tm, d), lambda b, i: (b, i, 0)),
        out_shape=jax.ShapeDtypeStruct(x.shape, F32),
        scratch_shapes=[pltpu.VMEM((tm, ch), F32), pltpu.VMEM((CONV_HALO, ch), F32),
                        pltpu.VMEM((SUBLANES, tm + CONV_HALO, ch), F32), pltpu.VMEM((tm, ch), F32)],
        compiler_params=_cparams(("parallel", "parallel")),
        name="conv_module",
    )(x, x, mod, row(g), w_pw1, row(b_pw1), wrep, row(b_dw), row(ln_g), row(ln_b), w_pw2, row(b_pw2))


def _rwkv_proj_kernel(x_ref, halo_ref, mod_ref, g_ref, mu_ref, wr_ref, wk_ref, wv_ref,
                      wa_ref, wb_ref, aa_ref, ab_ref, ga_ref, gb_ref, vec_ref,
                      r_ref, ld_ref, k_ref, v_ref, kk_ref, a_ref, gate_ref):
    i = pl.program_id(1)
    g = g_ref[...]
    scale, shift = mod_ref[1:2, :], mod_ref[0:1, :]
    h = _normmod(x_ref[...], g, scale, shift)
    h_halo = _normmod(halo_ref[...], g, scale, shift)
    last = jnp.where(i > 0, h_halo[7:8, :], jnp.zeros_like(h_halo[7:8, :]))
    rows = lax.broadcasted_iota(jnp.int32, h.shape, 0)
    h_prev = jnp.where(rows == 0, last, pltpu.roll(h, 1, 0))
    xx = h_prev - h

    def mix(m):
        return (h + xx * mu_ref[m:m + 1, :]).astype(BF16)

    w0, a0, k_k, k_a = vec_ref[0:1, :], vec_ref[1:2, :], vec_ref[2:3, :], vec_ref[3:4, :]
    r = _dot(mix(0), wr_ref[...])
    k = _dot(mix(2), wk_ref[...])
    v = _dot(mix(3), wv_ref[...])
    zw = w0 + _dot(jnp.tanh(_dot(mix(1), wa_ref[...])).astype(BF16), wb_ref[...])
    w = -(jnp.maximum(-zw, 0.0) + jnp.log1p(jnp.exp(-jnp.abs(zw)))) - 0.5
    a = _sigmoid(a0 + _dot(_dot(mix(4), aa_ref[...]).astype(BF16), ab_ref[...]))
    gate = _dot(_sigmoid(_dot(mix(5), ga_ref[...])).astype(BF16), gb_ref[...])
    kk = k * k_k
    ones = _head_ones(LANE_GROUP)
    sq = (kk * kk).astype(BF16)
    d = kk.shape[1]
    ss = jnp.concatenate([_dot(sq[:, c:c + LANE_GROUP], ones) for c in range(0, d, LANE_GROUP)], axis=1)
    kk = kk / jnp.maximum(jnp.sqrt(ss), 1e-12)
    r_ref[...] = r.astype(r_ref.dtype)
    ld_ref[...] = -jnp.exp(w)
    k_ref[...] = (k * (1.0 + (a - 1.0) * k_a)).astype(k_ref.dtype)
    v_ref[...] = v.astype(v_ref.dtype)
    kk_ref[...] = kk.astype(kk_ref.dtype)
    a_ref[...] = a.astype(a_ref.dtype)
    gate_ref[...] = gate.astype(gate_ref.dtype)


def _rwkv_proj(x, mod, g, mu, w_r, w_k, w_v, wa, wb, aa, ab, ga, gb, vecs):
    bsz, s, d = x.shape
    tm = 256
    ratio = tm // 8
    full = lambda a: pl.BlockSpec(a.shape, lambda b, i: (0,) * a.ndim)
    tok = pl.BlockSpec((None, tm, d), lambda b, i: (b, i, 0))
    mu8 = jnp.zeros((8, d), F32).at[:6].set(mu)
    weights = [w_r, w_k, w_v, wa, wb, aa, ab, ga, gb, vecs]
    return pl.pallas_call(
        _rwkv_proj_kernel,
        grid=(bsz, s // tm),
        in_specs=[tok,
                  pl.BlockSpec((None, 8, d), lambda b, i: (b, jnp.maximum(i * ratio - 1, 0), 0)),
                  pl.BlockSpec((None, 6, d), lambda b, i: (b, 0, 0)),
                  pl.BlockSpec((1, d), lambda b, i: (0, 0)),
                  full(mu8)] + [full(a) for a in weights],
        out_specs=[tok] * 7,
        out_shape=[jax.ShapeDtypeStruct(x.shape, F32 if i == 1 else BF16) for i in range(7)],
        compiler_params=_cparams(("parallel", "parallel")),
        name="rwkv_proj",
    )(x, x, mod, g.reshape(1, d), mu8, *weights)


def _wkv_kernel(r_ref, ld_ref, k_ref, v_ref, kk_ref, a_ref, gate_ref, vec_ref, o_ref, state):
    t = pl.program_id(1)

    @pl.when(t == 0)
    def _():
        state[...] = jnp.zeros_like(state)

    c = CHUNK
    w = LANE_GROUP
    nh = w // HEAD_DIM
    n_streams = r_ref.shape[0]
    n_chunks = r_ref.shape[1] // c
    n_groups = r_ref.shape[2] // w
    lane_sl = [slice(g * w, (g + 1) * w) for g in range(n_groups)]
    assert c == HEAD_DIM

    lane_head = _head_of((c, w), 1)
    head_masks = [lane_head == h for h in range(nh)]
    row_c = lax.broadcasted_iota(jnp.int32, (c, w), 0)
    col_in = _pos_in_head((c, w), 1)
    strict = col_in < row_c
    incl = col_in <= row_c
    eye = (col_in == row_c).astype(F32)
    tri = (lax.broadcasted_iota(jnp.int32, (c, c), 1) <= lax.broadcasted_iota(jnp.int32, (c, c), 0)).astype(BF16)
    diag_blocks = _head_of((w, w), 0) == _head_of((w, w), 1)
    ones = diag_blocks.astype(BF16)
    zero16 = jnp.zeros((), BF16)

    def bd(m):
        m16 = m.astype(BF16)
        return jnp.concatenate([jnp.where(msk, m16, zero16) for msk in head_masks], axis=0)

    def bd_cols(m):
        m16 = m.astype(BF16)
        return jnp.where(diag_blocks, jnp.concatenate([m16] * nh, axis=0), zero16)

    def mm(a, b):
        return _dot(a.astype(BF16), b.astype(BF16))

    def each(fn, *lists):
        return [fn(*args) for args in zip(*lists)]

    def body(ci, carry):
        rows = pl.ds(pl.multiple_of(ci * c, c), c)
        units = [(s, lane_sl[g]) for s in range(n_streams) for g in range(n_groups)]
        ld = [ld_ref[s, rows, ln] for s, ln in units]
        kk = [kk_ref[s, rows, ln].astype(F32) for s, ln in units]
        v = [v_ref[s, rows, ln].astype(F32) for s, ln in units]

        h1 = each(lambda x: x.astype(BF16), ld)
        h2 = each(lambda x, h: (x - h.astype(F32)).astype(BF16), ld, h1)
        c1 = each(lambda h: _dot(tri, h), h1)
        c2 = each(lambda h: _dot(tri, h), h2)
        cum = each(lambda x, y: x + y, c1, c2)
        p_in = each(jnp.exp, cum)
        p_inv = each(lambda x: jnp.exp(-x), cum)
        p_end = each(lambda p: p[c - 1:c, :], p_in)
        at = each(lambda q, x, l: -q * jnp.exp(x - l), kk, cum, ld)
        bt = [q * a_ref[s, rows, ln].astype(F32) * pi for (s, ln), q, pi in zip(units, kk, p_inv)]
        kt = [k_ref[s, rows, ln].astype(F32) * pi for (s, ln), pi in zip(units, p_inv)]
        lhs = [jnp.concatenate([x, r_ref[s, rows, ln].astype(F32) * p], axis=0).astype(BF16)
               for (s, ln), x, p in zip(units, at, p_in)]
        bt_bd = each(bd, bt)
        kt_bd = each(bd, kt)
        g_b = each(_dot_t, lhs, bt_bd)
        g_k = each(_dot_t, lhs, kt_bd)
        a_ab = each(lambda g: jnp.where(strict, g[:c], 0.0), g_b)
        a_ak = each(lambda g: jnp.where(strict, g[:c], 0.0).astype(BF16), g_k)
        b_cat = each(lambda gb, gk: jnp.concatenate(
            [jnp.where(incl, gb[c:], 0.0), jnp.where(incl, gk[c:], 0.0)], axis=1).astype(BF16), g_b, g_k)

        x_inv = each(lambda m: eye + m, a_ab)
        pw = each(lambda m: mm(m, bd_cols(m)), a_ab)
        for _ in range(4):
            both = each(lambda x, p: mm(jnp.concatenate([x, p], axis=0), bd_cols(p)), x_inv, pw)
            x_inv = each(lambda x, b: x + b[:c], x_inv, both)
            pw = each(lambda b: b[c:], both)
        x_inv = each(lambda x, p: (x + mm(x, bd_cols(p))).astype(BF16), x_inv, pw)

        v_bd = each(bd, v)
        akv = each(_dot, a_ak, v_bd)
        bk = each(lambda b, k, p: jnp.concatenate([b * p, k * p], axis=0).astype(BF16), bt, kt, p_end)
        rkr = [(r_ref[s, rows, ln].astype(F32) * k_ref[s, rows, ln].astype(F32) * vec_ref[0:1, ln]).astype(BF16)
               for s, ln in units]
        bonus = each(lambda x, val: _dot(x, ones) * val, rkr, v)

        s_bd = [state[s, g] for s in range(n_streams) for g in range(n_groups)]
        sa_sr = each(lambda l, st: _dot_t(l, st.astype(BF16)), lhs, s_bd)
        u_rhs = each(lambda x, y: bd(x[:c] + y), sa_sr, akv)
        u = each(_dot, x_inv, u_rhs)
        uv_t = each(lambda x, val: jnp.concatenate([x, val], axis=0).T.astype(BF16), u, v)
        upd = each(_dot, uv_t, bk)
        s_new = each(lambda st, p, x: st * p + jnp.where(diag_blocks, x, 0.0), s_bd, p_end, upd)
        i = 0
        for s in range(n_streams):
            for g in range(n_groups):
                state[s, g] = s_new[i]
                i += 1
        y_rhs = each(lambda x, vb: jnp.concatenate([bd(x), vb], axis=0), u, v_bd)
        y2 = each(_dot, b_cat, y_rhs)
        y = each(lambda x, z: x[c:] + z, sa_sr, y2)

        inv_n = 1.0 / HEAD_DIM
        mean = each(lambda x: _dot(x.astype(BF16), ones) * inv_n, y)
        yc = each(lambda x, mu: x - mu, y, mean)
        var = each(lambda x: _dot((x * x).astype(BF16), ones) * inv_n, yc)
        for (s, ln), x, vr, bo in zip(units, yc, var, bonus):
            yn = x * lax.rsqrt(vr + RWKV_GN_EPS) * vec_ref[1:2, ln] + vec_ref[2:3, ln]
            o_ref[s, rows, ln] = ((yn + bo) * gate_ref[s, rows, ln].astype(F32)).astype(o_ref.dtype)
        return carry

    lax.fori_loop(0, n_chunks, body, 0)


WKV_TOKENS = 256


def _wkv_scan(r, ld, k, v, kk, a, gate, vecs):
    bsz, s, d = r.shape
    tb, lw = WKV_TOKENS, d
    tok = pl.BlockSpec((bsz, tb, lw), lambda gi, t: (0, t, gi))
    return pl.pallas_call(
        _wkv_kernel,
        grid=(d // lw, s // tb),
        in_specs=[tok] * 7 + [pl.BlockSpec((8, lw), lambda gi, t: (0, gi))],
        out_specs=tok,
        out_shape=jax.ShapeDtypeStruct((bsz, s, d), BF16),
        scratch_shapes=[pltpu.VMEM((bsz, lw // LANE_GROUP, LANE_GROUP, LANE_GROUP), F32)],
        compiler_params=_cparams(("parallel", "arbitrary")),
        name="wkv7_scan",
    )(r, ld, k, v, kk, a, gate, vecs)


def _rwkv_layer(x, mod, g, mu, w_r, w_k, w_v, w_o, w0, wa, wb, a0, aa, ab, ga, gb, k_k, k_a, r_k, ln_g, ln_b):
    d = x.shape[-1]
    vecs = jnp.zeros((8, d), F32).at[0].set(w0).at[1].set(a0).at[2].set(k_k).at[3].set(k_a)
    r, ld, k, v, kk, a, gate = _rwkv_proj(x, mod, g, mu, w_r, w_k, w_v, wa, wb, aa, ab, ga, gb, vecs)
    svecs = jnp.zeros((8, d), F32).at[0].set(r_k.reshape(d)).at[1].set(ln_g).at[2].set(ln_b)
    z = _wkv_scan(r, ld, k, v, kk, a, gate, svecs)
    return _proj_residual(z, w_o, jnp.zeros((d,), F32), x, mod)


def _rope_tables(positions):
    bsz, s = positions.shape
    half = HEAD_DIM // 2
    inv_freq = jnp.power(ROPE_THETA, -jnp.arange(0, HEAD_DIM, 2, dtype=F32) / HEAD_DIM)
    ang = (positions.astype(F32)[..., None] * inv_freq).reshape(bsz, s * half // LANES, LANES)
    cos, sin = lax.optimization_barrier((jnp.cos(ang), jnp.sin(ang)))
    cos, sin = cos.reshape(bsz, s, half), sin.reshape(bsz, s, half)
    widen = lambda c, sn: (jnp.concatenate([c, c, c, c], -1), jnp.concatenate([-sn, sn, -sn, sn], -1))
    return widen(cos, sin), widen(_plane_major(cos), _plane_major(sin))


def _plane_major(t):
    bsz, s, n = t.shape
    return t.reshape(bsz, s // PLANES, PLANES, n).transpose(0, 2, 1, 3)


def kernel(x, c, positions, ada_w, ada_b, norm_mix_g, norm_ffn_g, ffn_w_gate, ffn_w_up, ffn_w_down, attn_w_qkv, attn_q_gain, attn_k_gain, attn_w_o, conv_w_pw1, conv_b_pw1, conv_w_dw, conv_b_dw, conv_ln_g, conv_ln_b, conv_w_pw2, conv_b_pw2, rwkv_mu, rwkv_w_r, rwkv_w_k, rwkv_w_v, rwkv_w_o, rwkv_w0, rwkv_w_lora_a, rwkv_w_lora_b, rwkv_a0, rwkv_a_lora_a, rwkv_a_lora_b, rwkv_g_lora_a, rwkv_g_lora_b, rwkv_k_k, rwkv_k_a, rwkv_r_k, rwkv_ln_g, rwkv_ln_b):
    depth = ada_w.shape[0]
    bf = lambda a: a.astype(BF16)
    (cos_t, sin_t), (cos_p, sin_p) = _rope_tables(positions)
    mods = _ada_mod(c, ada_w, ada_b)
    for i in range(depth):
        mod = mods[i]
        kind, j = i % 3, i // 3
        if kind == 0:
            x = _attention_layer(x, mod, norm_mix_g[i], bf(attn_w_qkv[j]), attn_q_gain[j], attn_k_gain[j],
                                 bf(attn_w_o[j]), cos_t, sin_t, cos_p, sin_p)
        elif kind == 1:
            x = _conv_layer(x, mod, norm_mix_g[i], bf(conv_w_pw1[j]), conv_b_pw1[j], conv_w_dw[j], conv_b_dw[j],
                            conv_ln_g[j], conv_ln_b[j], bf(conv_w_pw2[j]), conv_b_pw2[j])
        else:
            x = _rwkv_layer(x, mod, norm_mix_g[i], rwkv_mu[j], bf(rwkv_w_r[j]), bf(rwkv_w_k[j]), bf(rwkv_w_v[j]),
                            bf(rwkv_w_o[j]), rwkv_w0[j], bf(rwkv_w_lora_a[j]), bf(rwkv_w_lora_b[j]), rwkv_a0[j],
                            bf(rwkv_a_lora_a[j]), bf(rwkv_a_lora_b[j]), bf(rwkv_g_lora_a[j]), bf(rwkv_g_lora_b[j]),
                            rwkv_k_k[j], rwkv_k_a[j], rwkv_r_k[j], rwkv_ln_g[j], rwkv_ln_b[j])
        x = _ffn(x, mod, norm_ffn_g[i], bf(ffn_w_gate[i]), bf(ffn_w_up[i]), bf(ffn_w_down[i]))
    return x
```
